```python
import math
import jax, jax.numpy as jnp
from jax import lax
import numpy as np

D_MODEL = 1024
BATCH = 2
SEQ = 16384
DEPTH = 2

CHUNK = 64
Q_BLOCK = 128
HEAD_DIM = 64
SB_HEADS = 8
DIFF_HEADS = 4
SB_WIDTH = SB_HEADS * HEAD_DIM
DIFF_QK_WIDTH = DIFF_HEADS * 2 * HEAD_DIM
DIFF_V_WIDTH = DIFF_HEADS * 2 * HEAD_DIM
EVEN_IN_WIDTH = 3 * SB_WIDTH + 2 * DIFF_QK_WIDTH + DIFF_V_WIDTH
POOL_GROUPS = 4
POOL_WINDOWS = (2, 4, 8, 16)
POOL_WIDTH = D_MODEL // 2
POOL_GROUP_DIM = POOL_WIDTH // POOL_GROUPS
CONV_WIDTH = D_MODEL - POOL_WIDTH
CONV_KERNEL = 31
ODD_IN_WIDTH = POOL_WIDTH + 2 * CONV_WIDTH
D_FF = -(-(-(-8 * D_MODEL // 3)) // 256) * 256
ROPE_THETA = 10000.0
EPS = 1e-6
NEG = -1e30

kernel_name = 'chunk_causal_hybrid_sb_diff_pool_conv'


def rmsnorm(x, g):
    xf = x.astype(jnp.float32)
    y = xf * lax.rsqrt(jnp.mean(xf * xf, axis=-1, keepdims=True) + EPS)
    return (y * g.astype(jnp.float32)).astype(x.dtype)


def layernorm(x, g, b):
    xf = x.astype(jnp.float32)
    mu = jnp.mean(xf, axis=-1, keepdims=True)
    var = jnp.mean(jnp.square(xf - mu), axis=-1, keepdims=True)
    y = (xf - mu) * lax.rsqrt(var + EPS) * g.astype(jnp.float32) + b.astype(jnp.float32)
    return y.astype(x.dtype)


def rope(x, pos):
    d = x.shape[-1]
    inv = jnp.power(ROPE_THETA, -jnp.arange(0, d, 2, dtype=jnp.float32) / d)
    ang = pos.astype(jnp.float32)[:, None] * inv[None, :]
    cos = jnp.cos(ang)[None, :, None, :]
    sin = jnp.sin(ang)[None, :, None, :]
    xf = x.astype(jnp.float32)
    x1, x2 = xf[..., : d // 2], xf[..., d // 2:]
    return jnp.concatenate([x1 * cos - x2 * sin, x2 * cos + x1 * sin], axis=-1).astype(x.dtype)


def stick_breaking_attention(q, k, v):
    B, T, H, d = q.shape
    nb = T // Q_BLOCK
    scale = d ** -0.5
    qb = jnp.moveaxis(q.reshape(B, nb, Q_BLOCK, H, d), 1, 0)
    key_pos = jnp.arange(T)

    def block(args):
        qblk, i = args
        z = jnp.einsum('bqhd,bkhd->bhqk', qblk, k, preferred_element_type=jnp.float32) * scale
        qpos = i * Q_BLOCK + jnp.arange(Q_BLOCK)
        strict = key_pos[None, :] < qpos[:, None]
        log_1m = jnp.where(strict, jax.nn.log_sigmoid(-z), 0.0)
        after = lax.cumsum(log_1m, axis=3, reverse=True) - log_1m
        w = jnp.where(strict, jnp.exp(jax.nn.log_sigmoid(z) + after), 0.0)
        return jnp.einsum('bhqk,bkhd->bqhd', w.astype(v.dtype), v)

    out = lax.map(block, (qb, jnp.arange(nb)))
    return jnp.moveaxis(out, 0, 1).reshape(B, T, H, d)


def differential_attention(q, k, v, lam):
    B, T, H, _, d = q.shape
    nb = T // Q_BLOCK
    scale = d ** -0.5
    qb = jnp.moveaxis(q.reshape(B, nb, Q_BLOCK, H, 2, d), 1, 0)
    key_chunk = jnp.arange(T) // CHUNK

    def block(args):
        qblk, i = args
        z = jnp.einsum('bqhmd,bkhmd->bhmqk', qblk, k, preferred_element_type=jnp.float32) * scale
        q_chunk = (i * Q_BLOCK + jnp.arange(Q_BLOCK)) // CHUNK
        mask = key_chunk[None, :] <= q_chunk[:, None]
        p = jax.nn.softmax(jnp.where(mask, z, NEG), axis=-1)
        attn = p[:, :, 0] - lam * p[:, :, 1]
        return jnp.einsum('bhqk,bkhe->bqhe', attn.astype(v.dtype), v)

    out = lax.map(block, (qb, jnp.arange(nb)))
    return jnp.moveaxis(out, 0, 1).reshape(B, T, H, 2 * d)


def even_mixer(h, w_in, lq1, lk1, lq2, lk2, subln_g, w_out, lambda_init, pos):
    B, T, _ = h.shape
    proj = h @ w_in
    cuts = np.cumsum([SB_WIDTH, SB_WIDTH, SB_WIDTH, DIFF_QK_WIDTH, DIFF_QK_WIDTH]).tolist()
    sq, sk, sv, dq, dk, dv = jnp.split(proj, cuts, axis=-1)
    sq = sq.reshape(B, T, SB_HEADS, HEAD_DIM)
    sk = sk.reshape(B, T, SB_HEADS, HEAD_DIM)
    sv = sv.reshape(B, T, SB_HEADS, HEAD_DIM)
    a_out = stick_breaking_attention(sq, sk, sv).reshape(B, T, SB_WIDTH)
    dq = rope(dq.reshape(B, T, DIFF_HEADS * 2, HEAD_DIM), pos).reshape(B, T, DIFF_HEADS, 2, HEAD_DIM)
    dk = rope(dk.reshape(B, T, DIFF_HEADS * 2, HEAD_DIM), pos).reshape(B, T, DIFF_HEADS, 2, HEAD_DIM)
    dv = dv.reshape(B, T, DIFF_HEADS, 2 * HEAD_DIM)
    lam = (jnp.exp(jnp.sum(lq1.astype(jnp.float32) * lk1.astype(jnp.float32)))
           - jnp.exp(jnp.sum(lq2.astype(jnp.float32) * lk2.astype(jnp.float32))) + lambda_init)
    b_out = differential_attention(dq, dk, dv, lam)
    b_out = (rmsnorm(b_out, subln_g) * (1.0 - lambda_init)).reshape(B, T, DIFF_V_WIDTH)
    return jnp.concatenate([a_out, b_out], axis=-1) @ w_out


def multiscale_pool_minus_identity(x):
    B, T, _ = x.shape
    t1 = jnp.arange(1, T + 1)
    outs = []
    for g, w in enumerate(POOL_WINDOWS):
        xs = x[..., g * POOL_GROUP_DIM:(g + 1) * POOL_GROUP_DIM].astype(jnp.float32)
        cs = jnp.concatenate([jnp.zeros((B, 1, POOL_GROUP_DIM), jnp.float32),
                              jnp.cumsum(xs, axis=1)], axis=1)
        lo = jnp.maximum(t1 - w, 0)
        cnt = jnp.minimum(t1, w).astype(jnp.float32)
        outs.append((cs[:, 1:] - cs[:, lo]) / cnt[None, :, None] - xs)
    return jnp.concatenate(outs, axis=-1).astype(x.dtype)


def odd_mixer(h, w_in, pool_w, pool_scale, dw_w, dw_b, cn_g, cn_b, w_out):
    B, T, _ = h.shape
    proj = h @ w_in
    xp = proj[..., :POOL_WIDTH]
    xa = proj[..., POOL_WIDTH:POOL_WIDTH + CONV_WIDTH]
    xg = proj[..., POOL_WIDTH + CONV_WIDTH:]
    pooled = multiscale_pool_minus_identity(xp).reshape(B, T, POOL_GROUPS, POOL_GROUP_DIM)
    c_out = jnp.einsum('btgi,gio->btgo', pooled, pool_w).reshape(B, T, POOL_WIDTH) * pool_scale
    u = xa * jax.nn.sigmoid(xg)
    u = lax.conv_general_dilated(u, dw_w[:, None, :].astype(u.dtype), window_strides=(1,),
                                 padding=[(CONV_KERNEL - 1, 0)],
                                 dimension_numbers=('NWC', 'WIO', 'NWC'),
                                 feature_group_count=CONV_WIDTH) + dw_b
    d_out = jax.nn.silu(layernorm(u, cn_g, cn_b))
    return jnp.concatenate([c_out, d_out], axis=-1) @ w_out


def swiglu_ffn(h, w_gate, w_up, w_down):
    return (jax.nn.silu(h @ w_gate) * (h @ w_up)) @ w_down


def setup_inputs(seed: int = 0) -> dict:
    key = jax.random.key(seed)
    ks = iter(jax.random.split(key, 40))

    def nrm(shape, scale):
        return jax.random.normal(next(ks), shape, jnp.float32) * scale

    def gain(n):
        return 1.0 + nrm((n,), 0.02)

    d = D_MODEL
    return {
        'x': nrm((BATCH, SEQ, d), 1.0),
        'mix_norm_0': gain(d),
        'w_in_0': nrm((d, EVEN_IN_WIDTH), d ** -0.5),
        'lambda_q1_0': nrm((HEAD_DIM,), 0.1),
        'lambda_k1_0': nrm((HEAD_DIM,), 0.1),
        'lambda_q2_0': nrm((HEAD_DIM,), 0.1),
        'lambda_k2_0': nrm((HEAD_DIM,), 0.1),
        'subln_0': gain(2 * HEAD_DIM),
        'w_out_0': nrm((SB_WIDTH + DIFF_V_WIDTH, d), (SB_WIDTH + DIFF_V_WIDTH) ** -0.5),
        'ffn_norm_0': gain(d),
        'w_gate_0': nrm((d, D_FF), d ** -0.5),
        'w_up_0': nrm((d, D_FF), d ** -0.5),
        'w_down_0': nrm((D_FF, d), D_FF ** -0.5),
        'mix_norm_1': gain(d),
        'w_in_1': nrm((d, ODD_IN_WIDTH), d ** -0.5),
        'pool_w_1': nrm((POOL_GROUPS, POOL_GROUP_DIM, POOL_GROUP_DIM), POOL_GROUP_DIM ** -0.5),
        'pool_scale_1': gain(POOL_WIDTH),
        'dw_w_1': nrm((CONV_KERNEL, CONV_WIDTH), CONV_KERNEL ** -0.5),
        'dw_b_1': nrm((CONV_WIDTH,), 0.01),
        'conv_norm_g_1': gain(CONV_WIDTH),
        'conv_norm_b_1': nrm((CONV_WIDTH,), 0.01),
        'w_out_1': nrm((POOL_WIDTH + CONV_WIDTH, d), (POOL_WIDTH + CONV_WIDTH) ** -0.5),
        'ffn_norm_1': gain(d),
        'w_gate_1': nrm((d, D_FF), d ** -0.5),
        'w_up_1': nrm((d, D_FF), d ** -0.5),
        'w_down_1': nrm((D_FF, d), D_FF ** -0.5),
        'final_norm': gain(d),
    }


def reference(x, mix_norm_0, w_in_0, lambda_q1_0, lambda_k1_0, lambda_q2_0, lambda_k2_0,
              subln_0, w_out_0, ffn_norm_0, w_gate_0, w_up_0, w_down_0,
              mix_norm_1, w_in_1, pool_w_1, pool_scale_1, dw_w_1, dw_b_1,
              conv_norm_g_1, conv_norm_b_1, w_out_1, ffn_norm_1, w_gate_1, w_up_1,
              w_down_1, final_norm):
    T = x.shape[1]
    pos = jnp.arange(T, dtype=jnp.int32)
    even_params = [(mix_norm_0, w_in_0, lambda_q1_0, lambda_k1_0, lambda_q2_0, lambda_k2_0,
                    subln_0, w_out_0)]
    odd_params = [(mix_norm_1, w_in_1, pool_w_1, pool_scale_1, dw_w_1, dw_b_1,
                   conv_norm_g_1, conv_norm_b_1, w_out_1)]
    ffn_params = [(ffn_norm_0, w_gate_0, w_up_0, w_down_0),
                  (ffn_norm_1, w_gate_1, w_up_1, w_down_1)]
    for i in range(DEPTH):
        if i % 2 == 0:
            p = even_params[i // 2]
            lambda_init = 0.8 - 0.6 * math.exp(-0.3 * i)
            x = x + even_mixer(rmsnorm(x, p[0]), *p[1:], lambda_init, pos)
        else:
            p = odd_params[i // 2]
            x = x + odd_mixer(rmsnorm(x, p[0]), *p[1:])
        f = ffn_params[i]
        x = x + swiglu_ffn(rmsnorm(x, f[0]), *f[1:])
    return rmsnorm(x, final_norm)
```

```python
import functools
import math

import jax
import jax.numpy as jnp
from jax import lax
from jax.experimental import pallas as pl
from jax.experimental.pallas import tpu as pltpu

F32 = jnp.float32
BF16 = jnp.bfloat16

D_MODEL = 1024
HEAD_DIM = 64
LANES = 128
SB_HEADS = 8
DIFF_HEADS = 4
SB_WIDTH = SB_HEADS * HEAD_DIM
DIFF_WIDTH = DIFF_HEADS * 2 * HEAD_DIM
EVEN_IN_WIDTH = 3 * SB_WIDTH + 3 * DIFF_WIDTH
CHUNK = 64
POOL_WINDOWS = (2, 4, 8, 16)
POOL_WIDTH = 512
POOL_GROUP_DIM = 128
CONV_WIDTH = 512
CONV_KERNEL = 31
ODD_IN_WIDTH = POOL_WIDTH + 2 * CONV_WIDTH
D_FF = 2816
FF_CHUNK = 256
ROPE_THETA = 10000.0
EPS = 1e-6
NEG = -1e30
HALO = 32
SB_LOG_ZERO = -150.0
VMEM_LIMIT = 56 * 1024 * 1024


def _rms(x, g):
    return x * lax.rsqrt(jnp.mean(x * x, axis=-1, keepdims=True) + EPS) * g


def _sigmoid(x):
    return 1.0 / (1.0 + jnp.exp(-x))


def _dot(a, b):
    return jnp.dot(a, b, preferred_element_type=F32)


def _dot_nt(a, b):
    return lax.dot_general(a, b, (((1,), (1,)), ((), ())), preferred_element_type=F32)


def _params(*sem):
    return pltpu.CompilerParams(dimension_semantics=sem, vmem_limit_bytes=VMEM_LIMIT)


def _const_spec(shape):
    nd = len(shape)
    return pl.BlockSpec(shape, lambda *_: (0,) * nd)


def _in_proj0_kernel(x_ref, g_ref, w_ref, cos_ref, sin_ref, o_ref):
    h = _rms(x_ref[...], g_ref[...]).astype(BF16)
    tm = h.shape[0]
    lane = lax.broadcasted_iota(jnp.int32, (tm, LANES), 1)
    first_half = (lane % HEAD_DIM) < (HEAD_DIM // 2)
    cos = cos_ref[...]
    sin = sin_ref[...]
    scale = HEAD_DIM ** -0.5
    seg = SB_WIDTH
    for c in range(EVEN_IN_WIDTH // seg):
        y = _dot(h, w_ref[:, c * seg:(c + 1) * seg])
        is_rope = c in (3, 4)
        is_query = c in (0, 3)
        for s in range(seg // LANES):
            t = y[:, s * LANES:(s + 1) * LANES]
            if is_rope:
                swapped = jnp.where(first_half,
                                    pltpu.roll(t, LANES - HEAD_DIM // 2, 1),
                                    pltpu.roll(t, HEAD_DIM // 2, 1))
                t = t * cos + swapped * sin
            if is_query:
                t = t * scale
            o_ref[:, c * seg + s * LANES:c * seg + (s + 1) * LANES] = t.astype(BF16)


def _in_proj0(x2, g, w, cos_t, sin_t, seq, tm):
    n = x2.shape[0]
    per_seq = seq // tm
    return pl.pallas_call(
        _in_proj0_kernel,
        grid=(n // tm,),
        in_specs=[
            pl.BlockSpec((tm, D_MODEL), lambda i: (i, 0)),
            _const_spec((1, D_MODEL)),
            _const_spec((D_MODEL, EVEN_IN_WIDTH)),
            pl.BlockSpec((tm, LANES), lambda i: (i % per_seq, 0)),
            pl.BlockSpec((tm, LANES), lambda i: (i % per_seq, 0)),
        ],
        out_specs=pl.BlockSpec((tm, EVEN_IN_WIDTH), lambda i: (i, 0)),
        out_shape=jax.ShapeDtypeStruct((n, EVEN_IN_WIDTH), BF16),
        compiler_params=_params("arbitrary"),
        name="l0_in_proj",
    )(x2, g, w, cos_t, sin_t)


def _sb_kernel(q_ref, k_ref, v_ref, o_ref, acc_ref, rest_ref, *, tq):
    i = pl.program_id(2)
    q = q_ref[...]
    lane = lax.broadcasted_iota(jnp.int32, (tq, LANES), 1)
    low = lane < HEAD_DIM
    zero = jnp.zeros_like(q)
    q_heads = (jnp.where(low, q, zero), jnp.where(low, zero, q))
    tri = (lax.broadcasted_iota(jnp.int32, (tq, tq), 0)
           > lax.broadcasted_iota(jnp.int32, (tq, tq), 1)).astype(BF16)
    strict = (lax.broadcasted_iota(jnp.int32, (tq, tq), 1)
              < lax.broadcasted_iota(jnp.int32, (tq, tq), 0))

    acc_ref[...] = jnp.zeros_like(acc_ref)
    rest_ref[...] = jnp.zeros_like(rest_ref)

    def block(j, diagonal):
        start = pl.multiple_of(j * tq, tq)
        k = k_ref[pl.ds(start, tq), :]
        v = v_ref[pl.ds(start, tq), :]
        worst = None
        for hd in range(2):
            z = _dot_nt(q_heads[hd], k)
            log_1m = -(jnp.maximum(z, 0.0) + jnp.log(1.0 + jnp.exp(-jnp.abs(z))))
            if diagonal:
                log_1m = jnp.where(strict, log_1m, 0.0)
            hi = log_1m.astype(BF16)
            lo = (log_1m - hi.astype(F32)).astype(BF16)
            after = _dot(hi, tri) + _dot(lo, tri)
            rest = rest_ref[hd]
            w = jnp.exp(z + log_1m + after + rest)
            if diagonal:
                w = jnp.where(strict, w, 0.0)
            acc_ref[hd] += _dot(w.astype(BF16), v)
            rest = rest + jnp.sum(log_1m, axis=1, keepdims=True)
            rest_ref[hd] = rest
            m = jnp.max(rest)
            worst = m if worst is None else jnp.maximum(worst, m)
        return worst

    worst0 = block(i, True)

    def cond(st):
        j, worst = st
        return jnp.logical_and(j >= 0, worst > SB_LOG_ZERO)

    def body(st):
        j, _ = st
        return j - 1, block(j, False)

    lax.while_loop(cond, body, (i - 1, worst0))
    o_ref[...] = jnp.where(low, acc_ref[0], acc_ref[1]).astype(o_ref.dtype)


def _sb_attention(proj, batch, seq, tq):
    n = proj.shape[0]
    nq = seq // tq
    q_off = 0
    k_off = SB_WIDTH // LANES
    v_off = 2 * SB_WIDTH // LANES
    return pl.pallas_call(
        functools.partial(_sb_kernel, tq=tq),
        grid=(batch, SB_WIDTH // LANES, nq),
        in_specs=[
            pl.BlockSpec((tq, LANES), lambda b, p, i: (b * nq + i, q_off + p)),
            pl.BlockSpec((seq, LANES), lambda b, p, i: (b, k_off + p)),
            pl.BlockSpec((seq, LANES), lambda b, p, i: (b, v_off + p)),
        ],
        out_specs=pl.BlockSpec((tq, LANES), lambda b, p, i: (b * nq + i, p)),
        out_shape=jax.ShapeDtypeStruct((n, SB_WIDTH), BF16),
        scratch_shapes=[pltpu.VMEM((2, tq, LANES), F32), pltpu.VMEM((2, tq, 1), F32)],
        compiler_params=_params("arbitrary", "arbitrary", "arbitrary"),
        name="sb_attention",
    )(proj, proj, proj)


def _diff_kernel(lq1_ref, lk1_ref, lq2_ref, lk2_ref, g_ref, q_ref, k_ref, v_ref, o_ref,
                 m_ref, l_ref, acc_ref, *, tq, lambda_init):
    i = pl.program_id(2)
    q = q_ref[...]
    lane = lax.broadcasted_iota(jnp.int32, (tq, LANES), 1)
    low = lane < HEAD_DIM
    zero = jnp.zeros_like(q)
    q_maps = (jnp.where(low, q, zero), jnp.where(low, zero, q))
    visible = ((lax.broadcasted_iota(jnp.int32, (tq, tq), 1) // CHUNK)
               <= (lax.broadcasted_iota(jnp.int32, (tq, tq), 0) // CHUNK))

    m_ref[...] = jnp.full_like(m_ref, NEG)
    l_ref[...] = jnp.zeros_like(l_ref)
    acc_ref[...] = jnp.zeros_like(acc_ref)

    def block(j, diagonal):
        start = pl.multiple_of(j * tq, tq)
        k = k_ref[pl.ds(start, tq), :]
        v = v_ref[pl.ds(start, tq), :]
        for mp in range(2):
            s = _dot_nt(q_maps[mp], k)
            if diagonal:
                s = jnp.where(visible, s, NEG)
            m_old = m_ref[mp]
            m_new = jnp.maximum(m_old, jnp.max(s, axis=1, keepdims=True))
            p = jnp.exp(s - m_new)
            alpha = jnp.exp(m_old - m_new)
            l_ref[mp] = alpha * l_ref[mp] + jnp.sum(p, axis=1, keepdims=True)
            acc_ref[mp] = alpha * acc_ref[mp] + _dot(p.astype(BF16), v)
            m_ref[mp] = m_new

    def body(j, carry):
        block(j, False)
        return carry

    lax.fori_loop(0, i, body, 0)
    block(i, True)

    lam = (jnp.exp(jnp.sum(lq1_ref[...] * lk1_ref[...], axis=1, keepdims=True))
           - jnp.exp(jnp.sum(lq2_ref[...] * lk2_ref[...], axis=1, keepdims=True))
           + lambda_init)
    out = acc_ref[0] / l_ref[0] - lam * (acc_ref[1] / l_ref[1])
    out = _rms(out, g_ref[...]) * (1.0 - lambda_init)
    o_ref[...] = out.astype(o_ref.dtype)


def _diff_attention(proj, lq1, lk1, lq2, lk2, subln_g, batch, seq, tq, lambda_init):
    n = proj.shape[0]
    nq = seq // tq
    q_off = 3 * SB_WIDTH // LANES
    k_off = q_off + DIFF_WIDTH // LANES
    v_off = k_off + DIFF_WIDTH // LANES
    vec = _const_spec((1, HEAD_DIM))
    return pl.pallas_call(
        functools.partial(_diff_kernel, tq=tq, lambda_init=lambda_init),
        grid=(batch, DIFF_HEADS, nq),
        in_specs=[
            vec, vec, vec, vec,
            _const_spec((1, LANES)),
            pl.BlockSpec((tq, LANES), lambda b, h, i: (b * nq + i, q_off + h)),
            pl.BlockSpec((seq, LANES), lambda b, h, i: (b, k_off + h)),
            pl.BlockSpec((seq, LANES), lambda b, h, i: (b, v_off + h)),
        ],
        out_specs=pl.BlockSpec((tq, LANES), lambda b, h, i: (b * nq + i, h)),
        out_shape=jax.ShapeDtypeStruct((n, DIFF_WIDTH), BF16),
        scratch_shapes=[pltpu.VMEM((2, tq, 1), F32), pltpu.VMEM((2, tq, 1), F32),
                        pltpu.VMEM((2, tq, LANES), F32)],
        compiler_params=_params("arbitrary", "arbitrary", "arbitrary"),
        name="diff_attention",
    )(lq1, lk1, lq2, lk2, subln_g, proj, proj, proj)


def _out_ffn_kernel(x_ref, a_ref, b_ref, wo_ref, gf_ref, wgu_ref, wd_ref, gn_ref, o_ref,
                    acc_ref, *, final_norm):
    half = a_ref.shape[1]
    x1 = x_ref[...] + _dot(a_ref[...], wo_ref[:half, :]) + _dot(b_ref[...], wo_ref[half:, :])
    h = _rms(x1, gf_ref[...]).astype(BF16)
    acc_ref[...] = x1
    fc = wd_ref.shape[1]

    def body(c, carry):
        gu = _dot(h, wgu_ref[c])
        gate = gu[:, :fc]
        act = (gate * _sigmoid(gate) * gu[:, fc:]).astype(BF16)
        acc_ref[...] += _dot(act, wd_ref[c])
        return carry

    lax.fori_loop(0, wd_ref.shape[0], body, 0)
    y = acc_ref[...]
    if final_norm:
        y = _rms(y, gn_ref[...])
    o_ref[...] = y


def _out_ffn(x2, a, b, wo, gf, wgu, wd, gn, tm, final_norm):
    n = x2.shape[0]
    half = a.shape[1]
    nc, _, fc2 = wgu.shape
    return pl.pallas_call(
        functools.partial(_out_ffn_kernel, final_norm=final_norm),
        grid=(n // tm,),
        in_specs=[
            pl.BlockSpec((tm, D_MODEL), lambda i: (i, 0)),
            pl.BlockSpec((tm, half), lambda i: (i, 0)),
            pl.BlockSpec((tm, half), lambda i: (i, 0)),
            _const_spec((2 * half, D_MODEL)),
            _const_spec((1, D_MODEL)),
            _const_spec((nc, D_MODEL, fc2)),
            _const_spec((nc, fc2 // 2, D_MODEL)),
            _const_spec((1, D_MODEL)),
        ],
        out_specs=pl.BlockSpec((tm, D_MODEL), lambda i: (i, 0)),
        out_shape=jax.ShapeDtypeStruct((n, D_MODEL), F32),
        scratch_shapes=[pltpu.VMEM((tm, D_MODEL), F32)],
        compiler_params=_params("arbitrary"),
        name="out_proj_ffn",
    )(x2, a, b, wo, gf, wgu, wd, gn)


def _l1_mixer_kernel(x_ref, g_ref, w_ref, pw_ref, ps_ref, dw_ref, db_ref, cg_ref, cb_ref,
                     c_ref, d_ref, ext_p, ext_u):
    ti = pl.program_id(1)
    tm = x_ref.shape[0]

    @pl.when(ti == 0)
    def _():
        ext_p[:HALO, :] = jnp.zeros((HALO, POOL_WIDTH), F32)
        ext_u[:HALO, :] = jnp.zeros((HALO, CONV_WIDTH), F32)

    h = _rms(x_ref[...], g_ref[...]).astype(BF16)
    xp = _dot(h, w_ref[:, :POOL_WIDTH])
    xa = _dot(h, w_ref[:, POOL_WIDTH:POOL_WIDTH + CONV_WIDTH])
    xg = _dot(h, w_ref[:, POOL_WIDTH + CONV_WIDTH:])
    ext_p[HALO:, :] = xp
    ext_u[HALO:, :] = xa * _sigmoid(xg)

    pos1 = (ti * tm + 1 + lax.broadcasted_iota(jnp.int32, (tm, 1), 0)).astype(F32)
    for g, win in enumerate(POOL_WINDOWS):
        cols = slice(g * POOL_GROUP_DIM, (g + 1) * POOL_GROUP_DIM)
        total = ext_p[HALO:, cols]
        for back in range(1, win):
            total = total + ext_p[HALO - back:HALO - back + tm, cols]
        pooled = total / jnp.minimum(pos1, float(win)) - ext_p[HALO:, cols]
        mixed = _dot(pooled.astype(BF16), pw_ref[g])
        c_ref[:, cols] = (mixed * ps_ref[:, cols]).astype(c_ref.dtype)

    conv = jnp.zeros((tm, CONV_WIDTH), F32) + db_ref[...]
    for j in range(CONV_KERNEL):
        back = CONV_KERNEL - 1 - j
        conv = conv + ext_u[HALO - back:HALO - back + tm, :] * dw_ref[j:j + 1, :]
    mu = jnp.mean(conv, axis=-1, keepdims=True)
    cen = conv - mu
    var = jnp.mean(cen * cen, axis=-1, keepdims=True)
    y = cen * lax.rsqrt(var + EPS) * cg_ref[...] + cb_ref[...]
    d_ref[...] = (y * _sigmoid(y)).astype(d_ref.dtype)

    ext_p[:HALO, :] = ext_p[tm:, :]
    ext_u[:HALO, :] = ext_u[tm:, :]


def _l1_mixer(x2, g, w, pw, ps, dw, db, cg, cb, batch, seq, tm):
    n = x2.shape[0]
    nt = seq // tm
    row = lambda b, t: (b * nt + t, 0)
    return pl.pallas_call(
        _l1_mixer_kernel,
        grid=(batch, nt),
        in_specs=[
            pl.BlockSpec((tm, D_MODEL), row),
            _const_spec((1, D_MODEL)),
            _const_spec((D_MODEL, ODD_IN_WIDTH)),
            _const_spec((len(POOL_WINDOWS), POOL_GROUP_DIM, POOL_GROUP_DIM)),
            _const_spec((1, POOL_WIDTH)),
            _const_spec((CONV_KERNEL, CONV_WIDTH)),
            _const_spec((1, CONV_WIDTH)),
            _const_spec((1, CONV_WIDTH)),
            _const_spec((1, CONV_WIDTH)),
        ],
        out_specs=[pl.BlockSpec((tm, POOL_WIDTH), row), pl.BlockSpec((tm, CONV_WIDTH), row)],
        out_shape=[jax.ShapeDtypeStruct((n, POOL_WIDTH), BF16),
                   jax.ShapeDtypeStruct((n, CONV_WIDTH), BF16)],
        scratch_shapes=[pltpu.VMEM((HALO + tm, POOL_WIDTH), F32),
                        pltpu.VMEM((HALO + tm, CONV_WIDTH), F32)],
        compiler_params=_params("arbitrary", "arbitrary"),
        name="l1_mixer",
    )(x2, g, w, pw, ps, dw, db, cg, cb)


def _ffn_weights(w_gate, w_up, w_down):
    nc = D_FF // FF_CHUNK
    wg = w_gate.astype(BF16).reshape(D_MODEL, nc, FF_CHUNK)
    wu = w_up.astype(BF16).reshape(D_MODEL, nc, FF_CHUNK)
    wgu = jnp.concatenate([wg, wu], axis=2).transpose(1, 0, 2)
    wd = w_down.astype(BF16).reshape(nc, FF_CHUNK, D_MODEL)
    return wgu, wd


def _rope_tables(seq):
    half = HEAD_DIM // 2
    inv = jnp.power(ROPE_THETA, -jnp.arange(0, HEAD_DIM, 2, dtype=F32) / HEAD_DIM)
    ang = jnp.arange(seq, dtype=jnp.int32).astype(F32)[:, None] * inv[None, :]
    cos = jnp.cos(ang)
    sin = jnp.sin(ang)
    reps = LANES // HEAD_DIM
    return (jnp.tile(jnp.concatenate([cos, cos], axis=1), (1, reps)),
            jnp.tile(jnp.concatenate([-sin, sin], axis=1), (1, reps)))


def _tile(seq, want):
    return want if seq % want == 0 else seq


def kernel(x, mix_norm_0, w_in_0, lambda_q1_0, lambda_k1_0, lambda_q2_0, lambda_k2_0, subln_0, w_out_0, ffn_norm_0, w_gate_0, w_up_0, w_down_0, mix_norm_1, w_in_1, pool_w_1, pool_scale_1, dw_w_1, dw_b_1, conv_norm_g_1, conv_norm_b_1, w_out_1, ffn_norm_1, w_gate_1, w_up_1, w_down_1, final_norm):
    batch, seq, d = x.shape
    assert d == D_MODEL and seq % LANES == 0
    x2 = x.reshape(batch * seq, d)
    row = lambda v: v.reshape(1, -1).astype(F32)
    tm = _tile(seq, 512)

    cos_t, sin_t = _rope_tables(seq)
    proj = _in_proj0(x2, row(mix_norm_0), w_in_0.astype(BF16), cos_t, sin_t, seq, tm)
    a_out = _sb_attention(proj, batch, seq, _tile(seq, 128))
    lambda_init = 0.8 - 0.6 * math.exp(-0.3 * 0)
    b_out = _diff_attention(proj, row(lambda_q1_0), row(lambda_k1_0), row(lambda_q2_0),
                            row(lambda_k2_0), row(subln_0), batch, seq, _tile(seq, 512),
                            lambda_init)
    wgu0, wd0 = _ffn_weights(w_gate_0, w_up_0, w_down_0)
    x2 = _out_ffn(x2, a_out, b_out, w_out_0.astype(BF16), row(ffn_norm_0), wgu0, wd0,
                  row(final_norm), tm, final_norm=False)

    c_out, d_out = _l1_mixer(x2, row(mix_norm_1), w_in_1.astype(BF16), pool_w_1.astype(BF16),
                             row(pool_scale_1), dw_w_1.astype(F32), row(dw_b_1),
                             row(conv_norm_g_1), row(conv_norm_b_1), batch, seq, tm)
    wgu1, wd1 = _ffn_weights(w_gate_1, w_up_1, w_down_1)
    out = _out_ffn(x2, c_out, d_out, w_out_1.astype(BF16), row(ffn_norm_1), wgu1, wd1,
                   row(final_norm), tm, final_norm=True)
    return out.reshape(batch, seq, d)
```

```python
import functools
import math

import jax
import jax.numpy as jnp
from jax import lax
from jax.experimental import pallas as pl
from jax.experimental.pallas import tpu as pltpu

F32 = jnp.float32
BF16 = jnp.bfloat16

D_MODEL = 1024
HEAD_DIM = 64
LANES = 128
SB_HEADS = 8
DIFF_HEADS = 4
SB_WIDTH = SB_HEADS * HEAD_DIM
DIFF_WIDTH = DIFF_HEADS * 2 * HEAD_DIM
EVEN_IN_WIDTH = 3 * SB_WIDTH + 3 * DIFF_WIDTH
CHUNK = 64
POOL_WINDOWS = (2, 4, 8, 16)
POOL_WIDTH = 512
POOL_GROUP_DIM = 128
CONV_WIDTH = 512
CONV_KERNEL = 31
ODD_IN_WIDTH = POOL_WIDTH + 2 * CONV_WIDTH
D_FF = 2816
FF_CHUNK = 256
ROPE_THETA = 10000.0
EPS = 1e-6
NEG = -1e30
SB_SCALE = HEAD_DIM ** -0.5
LOG2E = math.log2(math.e)
ONES_ROWS = 16
HALO = 32
SB_LOG_ZERO = -150.0
VMEM_LIMIT = 56 * 1024 * 1024


def _rms(x, g):
    return x * lax.rsqrt(jnp.mean(x * x, axis=-1, keepdims=True) + EPS) * g


def _sigmoid(x):
    return 1.0 / (1.0 + jnp.exp(-x))


def _dot(a, b):
    return jnp.dot(a, b, preferred_element_type=F32)


def _dot_nt(a, b):
    return lax.dot_general(a, b, (((1,), (1,)), ((), ())), preferred_element_type=F32)


def _params(*sem):
    return pltpu.CompilerParams(dimension_semantics=sem, vmem_limit_bytes=VMEM_LIMIT)


def _const_spec(shape):
    nd = len(shape)
    return pl.BlockSpec(shape, lambda *_: (0,) * nd)


def _in_proj0_kernel(x_ref, g_ref, w_ref, cos_ref, sin_ref, o_ref, vt_ref):
    h = _rms(x_ref[...], g_ref[...]).astype(BF16)
    tm = h.shape[0]
    lane = lax.broadcasted_iota(jnp.int32, (tm, LANES), 1)
    first_half = (lane % HEAD_DIM) < (HEAD_DIM // 2)
    cos = cos_ref[...]
    sin = sin_ref[...]
    seg = SB_WIDTH
    for c in range(EVEN_IN_WIDTH // seg):
        y = _dot(h, w_ref[:, c * seg:(c + 1) * seg])
        is_rope = c in (3, 4)
        scale = {0: SB_SCALE, 3: SB_SCALE * LOG2E}.get(c)
        for s in range(seg // LANES):
            t = y[:, s * LANES:(s + 1) * LANES]
            if c == 5:
                vt_ref[0, s] = t.T.astype(BF16)
                continue
            if is_rope:
                swapped = jnp.where(first_half,
                                    pltpu.roll(t, LANES - HEAD_DIM // 2, 1),
                                    pltpu.roll(t, HEAD_DIM // 2, 1))
                t = t * cos + swapped * sin
            if scale is not None:
                t = t * scale
            o_ref[:, c * seg + s * LANES:c * seg + (s + 1) * LANES] = t.astype(BF16)


def _in_proj0(x2, g, w, cos_t, sin_t, batch, seq, tm):
    n = x2.shape[0]
    per_seq = seq // tm
    width = EVEN_IN_WIDTH - DIFF_WIDTH
    return pl.pallas_call(
        _in_proj0_kernel,
        grid=(n // tm,),
        in_specs=[
            pl.BlockSpec((tm, D_MODEL), lambda i: (i, 0)),
            _const_spec((1, D_MODEL)),
            _const_spec((D_MODEL, EVEN_IN_WIDTH)),
            pl.BlockSpec((tm, LANES), lambda i: (i % per_seq, 0)),
            pl.BlockSpec((tm, LANES), lambda i: (i % per_seq, 0)),
        ],
        out_specs=[
            pl.BlockSpec((tm, width), lambda i: (i, 0)),
            pl.BlockSpec((1, DIFF_HEADS, LANES, tm), lambda i: (i // per_seq, 0, 0, i % per_seq)),
        ],
        out_shape=[jax.ShapeDtypeStruct((n, width), BF16),
                   jax.ShapeDtypeStruct((batch, DIFF_HEADS, LANES, seq), BF16)],
        compiler_params=_params("arbitrary"),
        name="l0_in_proj",
    )(x2, g, w, cos_t, sin_t)


def _sb_kernel(q_ref, k_ref, v_ref, o_ref, acc_ref, rest_ref, *, tq):
    i = pl.program_id(2)
    q = q_ref[...]
    lane = lax.broadcasted_iota(jnp.int32, (tq, LANES), 1)
    low = lane < HEAD_DIM
    zero = jnp.zeros_like(q)
    q_heads = (jnp.where(low, q, zero), jnp.where(low, zero, q))
    tri = (lax.broadcasted_iota(jnp.int32, (tq, tq), 0)
           > lax.broadcasted_iota(jnp.int32, (tq, tq), 1)).astype(BF16)
    strict = (lax.broadcasted_iota(jnp.int32, (tq, tq), 1)
              < lax.broadcasted_iota(jnp.int32, (tq, tq), 0))

    acc_ref[...] = jnp.zeros_like(acc_ref)
    rest_ref[...] = jnp.zeros_like(rest_ref)

    def block(j, diagonal):
        start = pl.multiple_of(j * tq, tq)
        k = k_ref[pl.ds(start, tq), :]
        v = v_ref[pl.ds(start, tq), :]
        worst = None
        for hd in range(2):
            z = _dot_nt(q_heads[hd], k)
            log_1m = -(jnp.maximum(z, 0.0) + jnp.log(1.0 + jnp.exp(-jnp.abs(z))))
            if diagonal:
                log_1m = jnp.where(strict, log_1m, 0.0)
            hi = log_1m.astype(BF16)
            lo = (log_1m - hi.astype(F32)).astype(BF16)
            after = _dot(hi, tri) + _dot(lo, tri)
            rest = rest_ref[hd]
            w = jnp.exp(z + log_1m + after + rest)
            if diagonal:
                w = jnp.where(strict, w, 0.0)
            acc_ref[hd] += _dot(w.astype(BF16), v)
            rest = rest + jnp.sum(log_1m, axis=1, keepdims=True)
            rest_ref[hd] = rest
            m = jnp.max(rest)
            worst = m if worst is None else jnp.maximum(worst, m)
        return worst

    worst0 = block(i, True)

    def cond(st):
        j, worst = st
        return jnp.logical_and(j >= 0, worst > SB_LOG_ZERO)

    def body(st):
        j, _ = st
        return j - 1, block(j, False)

    lax.while_loop(cond, body, (i - 1, worst0))
    o_ref[...] = jnp.where(low, acc_ref[0], acc_ref[1]).astype(o_ref.dtype)


def _sb_attention(proj, batch, seq, tq):
    n = proj.shape[0]
    nq = seq // tq
    q_off = 0
    k_off = SB_WIDTH // LANES
    v_off = 2 * SB_WIDTH // LANES
    return pl.pallas_call(
        functools.partial(_sb_kernel, tq=tq),
        grid=(batch, SB_WIDTH // LANES, nq),
        in_specs=[
            pl.BlockSpec((tq, LANES), lambda b, p, i: (b * nq + i, q_off + p)),
            pl.BlockSpec((seq, LANES), lambda b, p, i: (b, k_off + p)),
            pl.BlockSpec((seq, LANES), lambda b, p, i: (b, v_off + p)),
        ],
        out_specs=pl.BlockSpec((tq, LANES), lambda b, p, i: (b * nq + i, p)),
        out_shape=jax.ShapeDtypeStruct((n, SB_WIDTH), BF16),
        scratch_shapes=[pltpu.VMEM((2, tq, LANES), F32), pltpu.VMEM((2, tq, 1), F32)],
        compiler_params=_params("arbitrary", "arbitrary", "arbitrary"),
        name="sb_attention",
    )(proj, proj, proj)


def _diff_kernel(lq1_ref, lk1_ref, lq2_ref, lk2_ref, g_ref, q_ref, k_ref, vt_ref, o_ref,
                 m_ref, acc_ref, s_ref, bmax_ref, *, tq, lambda_init):
    i = pl.program_id(2)
    q = q_ref[...]
    lane = lax.broadcasted_iota(jnp.int32, (tq, LANES), 1)
    low = lane < HEAD_DIM
    zero = jnp.zeros_like(q)
    q_maps = (jnp.where(low, q, zero), jnp.where(low, zero, q))
    ones = jnp.ones((ONES_ROWS, tq), BF16)

    m_ref[...] = jnp.full_like(m_ref, NEG)
    acc_ref[...] = jnp.zeros_like(acc_ref)

    def scores(j, diagonal):
        k = k_ref[pl.ds(pl.multiple_of(j * tq, tq), tq), :]
        for mp in range(2):
            s = _dot_nt(k, q_maps[mp])
            if diagonal:
                visible = ((lax.broadcasted_iota(jnp.int32, (tq, tq), 0) // CHUNK)
                           <= (lax.broadcasted_iota(jnp.int32, (tq, tq), 1) // CHUNK))
                s = jnp.where(visible, s, NEG)
            s_ref[mp] = s
            bmax_ref[mp] = jnp.max(s, axis=0, keepdims=True)

    def accumulate(j):
        vt = jnp.concatenate([vt_ref[:, pl.ds(pl.multiple_of(j * tq, tq), tq)], ones], axis=0)
        for mp in range(2):
            m_old = m_ref[mp]
            m_new = jnp.maximum(m_old, bmax_ref[mp])
            p = jnp.exp2(s_ref[mp] - m_new).astype(BF16)
            acc_ref[mp] = jnp.exp2(m_old - m_new) * acc_ref[mp] + _dot(vt, p)
            m_ref[mp] = m_new

    scores(i, True)

    def body(j, carry):
        accumulate(jnp.where(j == 0, i, j - 1))
        scores(j, False)
        return carry

    lax.fori_loop(0, i, body, 0)
    accumulate(jnp.where(i == 0, i, i - 1))

    lam = (jnp.exp(jnp.sum(lq1_ref[...] * lk1_ref[...], axis=1, keepdims=True))
           - jnp.exp(jnp.sum(lq2_ref[...] * lk2_ref[...], axis=1, keepdims=True))
           + lambda_init)
    acc1 = acc_ref[0]
    acc2 = acc_ref[1]
    out_t = (acc1[:LANES] * (1.0 / acc1[LANES:LANES + 1])
             - lam * (acc2[:LANES] * (1.0 / acc2[LANES:LANES + 1])))
    out = _rms(out_t.T, g_ref[...]) * (1.0 - lambda_init)
    o_ref[...] = out.astype(o_ref.dtype)


def _diff_attention(proj, vt, lq1, lk1, lq2, lk2, subln_g, batch, seq, tq, lambda_init):
    n = proj.shape[0]
    nq = seq // tq
    q_off = 3 * SB_WIDTH // LANES
    k_off = q_off + DIFF_WIDTH // LANES
    vec = _const_spec((1, HEAD_DIM))
    return pl.pallas_call(
        functools.partial(_diff_kernel, tq=tq, lambda_init=lambda_init),
        grid=(batch, DIFF_HEADS, nq),
        in_specs=[
            vec, vec, vec, vec,
            _const_spec((1, LANES)),
            pl.BlockSpec((tq, LANES), lambda b, h, i: (b * nq + i, q_off + h)),
            pl.BlockSpec((seq, LANES), lambda b, h, i: (b, k_off + h)),
            pl.BlockSpec((None, None, LANES, seq), lambda b, h, i: (b, h, 0, 0)),
        ],
        out_specs=pl.BlockSpec((tq, LANES), lambda b, h, i: (b * nq + i, h)),
        out_shape=jax.ShapeDtypeStruct((n, DIFF_WIDTH), BF16),
        scratch_shapes=[pltpu.VMEM((2, 1, tq), F32),
                        pltpu.VMEM((2, LANES + ONES_ROWS, tq), F32),
                        pltpu.VMEM((2, tq, tq), F32),
                        pltpu.VMEM((2, 1, tq), F32)],
        compiler_params=_params("arbitrary", "arbitrary", "arbitrary"),
        name="diff_attention",
    )(lq1, lk1, lq2, lk2, subln_g, proj, proj, vt)


def _out_ffn_kernel(x_ref, a_ref, b_ref, wo_ref, gf_ref, wgu_ref, wd_ref, gn_ref, o_ref,
                    acc_ref, *, final_norm):
    half = a_ref.shape[1]
    x1 = x_ref[...] + _dot(a_ref[...], wo_ref[:half, :]) + _dot(b_ref[...], wo_ref[half:, :])
    h = _rms(x1, gf_ref[...]).astype(BF16)
    acc_ref[...] = x1
    fc = wd_ref.shape[1]

    def body(c, carry):
        gu = _dot(h, wgu_ref[c])
        gate = gu[:, :fc]
        act = (gate * _sigmoid(gate) * gu[:, fc:]).astype(BF16)
        acc_ref[...] += _dot(act, wd_ref[c])
        return carry

    lax.fori_loop(0, wd_ref.shape[0], body, 0)
    y = acc_ref[...]
    if final_norm:
        y = _rms(y, gn_ref[...])
    o_ref[...] = y


def _out_ffn(x2, a, b, wo, gf, wgu, wd, gn, tm, final_norm):
    n = x2.shape[0]
    half = a.shape[1]
    nc, _, fc2 = wgu.shape
    return pl.pallas_call(
        functools.partial(_out_ffn_kernel, final_norm=final_norm),
        grid=(n // tm,),
        in_specs=[
            pl.BlockSpec((tm, D_MODEL), lambda i: (i, 0)),
            pl.BlockSpec((tm, half), lambda i: (i, 0)),
            pl.BlockSpec((tm, half), lambda i: (i, 0)),
            _const_spec((2 * half, D_MODEL)),
            _const_spec((1, D_MODEL)),
            _const_spec((nc, D_MODEL, fc2)),
            _const_spec((nc, fc2 // 2, D_MODEL)),
            _const_spec((1, D_MODEL)),
        ],
        out_specs=pl.BlockSpec((tm, D_MODEL), lambda i: (i, 0)),
        out_shape=jax.ShapeDtypeStruct((n, D_MODEL), F32),
        scratch_shapes=[pltpu.VMEM((tm, D_MODEL), F32)],
        compiler_params=_params("arbitrary"),
        name="out_proj_ffn",
    )(x2, a, b, wo, gf, wgu, wd, gn)


def _l1_mixer_kernel(x_ref, g_ref, w_ref, pw_ref, ps_ref, dw_ref, db_ref, cg_ref, cb_ref,
                     c_ref, d_ref, ext_p, ext_u):
    ti = pl.program_id(1)
    tm = x_ref.shape[0]

    @pl.when(ti == 0)
    def _():
        ext_p[:HALO, :] = jnp.zeros((HALO, POOL_WIDTH), F32)
        ext_u[:HALO, :] = jnp.zeros((HALO, CONV_WIDTH), F32)

    h = _rms(x_ref[...], g_ref[...]).astype(BF16)
    xp = _dot(h, w_ref[:, :POOL_WIDTH])
    xa = _dot(h, w_ref[:, POOL_WIDTH:POOL_WIDTH + CONV_WIDTH])
    xg = _dot(h, w_ref[:, POOL_WIDTH + CONV_WIDTH:])
    ext_p[HALO:, :] = xp
    ext_u[HALO:, :] = xa * _sigmoid(xg)

    pos1 = (ti * tm + 1 + lax.broadcasted_iota(jnp.int32, (tm, 1), 0)).astype(F32)
    for g, win in enumerate(POOL_WINDOWS):
        cols = slice(g * POOL_GROUP_DIM, (g + 1) * POOL_GROUP_DIM)
        total = ext_p[HALO:, cols]
        for back in range(1, win):
            total = total + ext_p[HALO - back:HALO - back + tm, cols]
        pooled = total / jnp.minimum(pos1, float(win)) - ext_p[HALO:, cols]
        mixed = _dot(pooled.astype(BF16), pw_ref[g])
        c_ref[:, cols] = (mixed * ps_ref[:, cols]).astype(c_ref.dtype)

    conv = jnp.zeros((tm, CONV_WIDTH), F32) + db_ref[...]
    for j in range(CONV_KERNEL):
        back = CONV_KERNEL - 1 - j
        conv = conv + ext_u[HALO - back:HALO - back + tm, :] * dw_ref[j:j + 1, :]
    mu = jnp.mean(conv, axis=-1, keepdims=True)
    cen = conv - mu
    var = jnp.mean(cen * cen, axis=-1, keepdims=True)
    y = cen * lax.rsqrt(var + EPS) * cg_ref[...] + cb_ref[...]
    d_ref[...] = (y * _sigmoid(y)).astype(d_ref.dtype)

    ext_p[:HALO, :] = ext_p[tm:, :]
    ext_u[:HALO, :] = ext_u[tm:, :]


def _l1_mixer(x2, g, w, pw, ps, dw, db, cg, cb, batch, seq, tm):
    n = x2.shape[0]
    nt = seq // tm
    row = lambda b, t: (b * nt + t, 0)
    return pl.pallas_call(
        _l1_mixer_kernel,
        grid=(batch, nt),
        in_specs=[
            pl.BlockSpec((tm, D_MODEL), row),
            _const_spec((1, D_MODEL)),
            _const_spec((D_MODEL, ODD_IN_WIDTH)),
            _const_spec((len(POOL_WINDOWS), POOL_GROUP_DIM, POOL_GROUP_DIM)),
            _const_spec((1, POOL_WIDTH)),
            _const_spec((CONV_KERNEL, CONV_WIDTH)),
            _const_spec((1, CONV_WIDTH)),
            _const_spec((1, CONV_WIDTH)),
            _const_spec((1, CONV_WIDTH)),
        ],
        out_specs=[pl.BlockSpec((tm, POOL_WIDTH), row), pl.BlockSpec((tm, CONV_WIDTH), row)],
        out_shape=[jax.ShapeDtypeStruct((n, POOL_WIDTH), BF16),
                   jax.ShapeDtypeStruct((n, CONV_WIDTH), BF16)],
        scratch_shapes=[pltpu.VMEM((HALO + tm, POOL_WIDTH), F32),
                        pltpu.VMEM((HALO + tm, CONV_WIDTH), F32)],
        compiler_params=_params("arbitrary", "arbitrary"),
        name="l1_mixer",
    )(x2, g, w, pw, ps, dw, db, cg, cb)


def _ffn_weights(w_gate, w_up, w_down):
    nc = D_FF // FF_CHUNK
    wg = w_gate.astype(BF16).reshape(D_MODEL, nc, FF_CHUNK)
    wu = w_up.astype(BF16).reshape(D_MODEL, nc, FF_CHUNK)
    wgu = jnp.concatenate([wg, wu], axis=2).transpose(1, 0, 2)
    wd = w_down.astype(BF16).reshape(nc, FF_CHUNK, D_MODEL)
    return wgu, wd


def _rope_tables(seq):
    half = HEAD_DIM // 2
    inv = jnp.power(ROPE_THETA, -jnp.arange(0, HEAD_DIM, 2, dtype=F32) / HEAD_DIM)
    ang = jnp.arange(seq, dtype=jnp.int32).astype(F32)[:, None] * inv[None, :]
    cos = jnp.cos(ang)
    sin = jnp.sin(ang)
    reps = LANES // HEAD_DIM
    return (jnp.tile(jnp.concatenate([cos, cos], axis=1), (1, reps)),
            jnp.tile(jnp.concatenate([-sin, sin], axis=1), (1, reps)))


def _tile(seq, want):
    return want if seq % want == 0 else seq


def kernel(x, mix_norm_0, w_in_0, lambda_q1_0, lambda_k1_0, lambda_q2_0, lambda_k2_0, subln_0, w_out_0, ffn_norm_0, w_gate_0, w_up_0, w_down_0, mix_norm_1, w_in_1, pool_w_1, pool_scale_1, dw_w_1, dw_b_1, conv_norm_g_1, conv_norm_b_1, w_out_1, ffn_norm_1, w_gate_1, w_up_1, w_down_1, final_norm):
    batch, seq, d = x.shape
    assert d == D_MODEL and seq % LANES == 0
    x2 = x.reshape(batch * seq, d)
    row = lambda v: v.reshape(1, -1).astype(F32)
    tm = _tile(seq, 512)

    cos_t, sin_t = _rope_tables(seq)
    proj, vt = _in_proj0(x2, row(mix_norm_0), w_in_0.astype(BF16), cos_t, sin_t, batch, seq, tm)
    a_out = _sb_attention(proj, batch, seq, _tile(seq, 128))
    lambda_init = 0.8 - 0.6 * math.exp(-0.3 * 0)
    b_out = _diff_attention(proj, vt, row(lambda_q1_0), row(lambda_k1_0), row(lambda_q2_0),
                            row(lambda_k2_0), row(subln_0), batch, seq, _tile(seq, 512),
                            lambda_init)
    wgu0, wd0 = _ffn_weights(w_gate_0, w_up_0, w_down_0)
    x2 = _out_ffn(x2, a_out, b_out, w_out_0.astype(BF16), row(ffn_norm_0), wgu0, wd0,
                  row(final_norm), tm, final_norm=False)

    c_out, d_out = _l1_mixer(x2, row(mix_norm_1), w_in_1.astype(BF16), pool_w_1.astype(BF16),
                             row(pool_scale_1), dw_w_1.astype(F32), row(dw_b_1),
                             row(conv_norm_g_1), row(conv_norm_b_1), batch, seq, tm)
    wgu1, wd1 = _ffn_weights(w_gate_1, w_up_1, w_down_1)
    out = _out_ffn(x2, c_out, d_out, w_out_1.astype(BF16), row(ffn_norm_1), wgu1, wd1,
                   row(final_norm), tm, final_norm=True)
    return out.reshape(batch, seq, d)
```

```python
import functools
import math

import jax
import jax.numpy as jnp
from jax import lax
from jax.experimental import pallas as pl
from jax.experimental.pallas import tpu as pltpu

F32 = jnp.float32
BF16 = jnp.bfloat16

D_MODEL = 1024
HEAD_DIM = 64
LANES = 128
SB_HEADS = 8
DIFF_HEADS = 4
SB_WIDTH = SB_HEADS * HEAD_DIM
DIFF_WIDTH = DIFF_HEADS * 2 * HEAD_DIM
EVEN_IN_WIDTH = 3 * SB_WIDTH + 3 * DIFF_WIDTH
CHUNK = 64
POOL_WINDOWS = (2, 4, 8, 16)
POOL_WIDTH = 512
POOL_GROUP_DIM = 128
CONV_WIDTH = 512
CONV_KERNEL = 31
ODD_IN_WIDTH = POOL_WIDTH + 2 * CONV_WIDTH
D_FF = 2816
FF_CHUNK = 256
ROPE_THETA = 10000.0
EPS = 1e-6
NEG = -1e30
SB_SCALE = HEAD_DIM ** -0.5
LOG2E = math.log2(math.e)
ONES_ROWS = 16
HALO = 32
QK_WIDTH = 2 * SB_WIDTH + 2 * DIFF_WIDTH
SB_LOG2_ZERO = -150.0 * math.log2(math.e)
VMEM_LIMIT = 56 * 1024 * 1024


def _rms(x, g):
    return x * lax.rsqrt(jnp.mean(x * x, axis=-1, keepdims=True) + EPS) * g


def _sigmoid(x):
    return 1.0 / (1.0 + jnp.exp(-x))


def _dot(a, b):
    return jnp.dot(a, b, preferred_element_type=F32)


def _dot_nt(a, b):
    return lax.dot_general(a, b, (((1,), (1,)), ((), ())), preferred_element_type=F32)


def _params(*sem):
    return pltpu.CompilerParams(dimension_semantics=sem, vmem_limit_bytes=VMEM_LIMIT)


def _const_spec(shape):
    nd = len(shape)
    return pl.BlockSpec(shape, lambda *_: (0,) * nd)


def _in_proj0_kernel(x_ref, g_ref, w_ref, cos_ref, sin_ref, o_ref, svt_ref, dvt_ref):
    h = _rms(x_ref[...], g_ref[...]).astype(BF16)
    tm = h.shape[0]
    lane = lax.broadcasted_iota(jnp.int32, (tm, LANES), 1)
    first_half = (lane % HEAD_DIM) < (HEAD_DIM // 2)
    cos = cos_ref[...]
    sin = sin_ref[...]
    seg = SB_WIDTH
    q_scale = SB_SCALE * LOG2E
    out_col = {0: 0, 1: 1, 3: 2, 4: 3}
    for c in range(EVEN_IN_WIDTH // seg):
        y = _dot(h, w_ref[:, c * seg:(c + 1) * seg])
        for s in range(seg // LANES):
            t = y[:, s * LANES:(s + 1) * LANES]
            if c in (2, 5):
                (svt_ref if c == 2 else dvt_ref)[0, s] = t.T.astype(BF16)
                continue
            if c in (3, 4):
                swapped = jnp.where(first_half,
                                    pltpu.roll(t, LANES - HEAD_DIM // 2, 1),
                                    pltpu.roll(t, HEAD_DIM // 2, 1))
                t = t * cos + swapped * sin
            if c in (0, 3):
                t = t * q_scale
            col = out_col[c] * seg + s * LANES
            o_ref[:, col:col + LANES] = t.astype(BF16)


def _in_proj0(x2, g, w, cos_t, sin_t, batch, seq, tm):
    n = x2.shape[0]
    per_seq = seq // tm
    vt_spec = pl.BlockSpec((1, SB_WIDTH // LANES, LANES, tm),
                           lambda i: (i // per_seq, 0, 0, i % per_seq))
    vt_shape = jax.ShapeDtypeStruct((batch, SB_WIDTH // LANES, LANES, seq), BF16)
    return pl.pallas_call(
        _in_proj0_kernel,
        grid=(n // tm,),
        in_specs=[
            pl.BlockSpec((tm, D_MODEL), lambda i: (i, 0)),
            _const_spec((1, D_MODEL)),
            _const_spec((D_MODEL, EVEN_IN_WIDTH)),
            pl.BlockSpec((tm, LANES), lambda i: (i % per_seq, 0)),
            pl.BlockSpec((tm, LANES), lambda i: (i % per_seq, 0)),
        ],
        out_specs=[pl.BlockSpec((tm, QK_WIDTH), lambda i: (i, 0)), vt_spec, vt_spec],
        out_shape=[jax.ShapeDtypeStruct((n, QK_WIDTH), BF16), vt_shape, vt_shape],
        compiler_params=_params("arbitrary"),
        name="l0_in_proj",
    )(x2, g, w, cos_t, sin_t)


def _sb_kernel(q_ref, k_ref, vt_ref, o_ref, acc_ref, rest_ref, *, tq):
    i = pl.program_id(2)
    q = q_ref[...]
    lane = lax.broadcasted_iota(jnp.int32, (tq, LANES), 1)
    low = lane < HEAD_DIM
    zero = jnp.zeros_like(q)
    q2 = jnp.concatenate([jnp.where(low, q, zero), jnp.where(low, zero, q)], axis=0)
    tri_row = lax.broadcasted_iota(jnp.int32, (tq + ONES_ROWS, tq), 0)
    tri_col = lax.broadcasted_iota(jnp.int32, (tq + ONES_ROWS, tq), 1)
    tri = jnp.logical_or(tri_col > tri_row, tri_row >= tq).astype(BF16)

    def sweep(first, nblk, diagonal):
        nk = nblk * tq
        start = pl.multiple_of(first * tq, tq)
        z = _dot_nt(k_ref[pl.ds(start, nk), :], q2)
        log_1m = jnp.minimum(-z, 0.0) - jnp.log2(1.0 + jnp.exp2(-jnp.abs(z)))
        if diagonal:
            key = lax.broadcasted_iota(jnp.int32, (nk, 2 * tq), 0) - (nk - tq)
            qry = jnp.bitwise_and(lax.broadcasted_iota(jnp.int32, (nk, 2 * tq), 1), tq - 1)
            strict = key < qry
            log_1m = jnp.where(strict, log_1m, 0.0)
        hi = log_1m.astype(BF16)
        lo = (log_1m - hi.astype(F32)).astype(BF16)
        rest = rest_ref[...]
        log_w = []
        for b in reversed(range(nblk)):
            rows = slice(b * tq, (b + 1) * tq)
            after = _dot(tri, hi[rows]) + _dot(tri, lo[rows])
            log_w.append(z[rows] + log_1m[rows] + after[:tq] + rest)
            rest = rest + after[tq:tq + 1]
        w = jnp.exp2(jnp.concatenate(log_w[::-1], axis=0))
        if diagonal:
            w = jnp.where(strict, w, 0.0)
        acc_ref[...] += _dot(vt_ref[:, pl.ds(start, nk)], w.astype(BF16))
        rest_ref[...] = rest

    acc_ref[...] = jnp.zeros_like(acc_ref)
    rest_ref[...] = jnp.zeros_like(rest_ref)

    @pl.when(i == 0)
    def _():
        sweep(0, 1, True)

    @pl.when(i > 0)
    def _():
        sweep(i - 1, 2, True)

    def cond(st):
        j, worst = st
        return jnp.logical_and(j >= 0, worst > SB_LOG2_ZERO)

    def body(st):
        j, _ = st
        sweep(j, 1, False)
        return j - 1, jnp.max(rest_ref[...])

    lax.while_loop(cond, body, (i - 2, jnp.max(rest_ref[...])))
    head0_rows = lax.broadcasted_iota(jnp.int32, (LANES, tq), 0) < HEAD_DIM
    acc = acc_ref[...]
    o_ref[...] = jnp.where(head0_rows, acc[:, :tq], acc[:, tq:]).T.astype(o_ref.dtype)


def _sb_attention(proj, vt, batch, seq, tq):
    n = proj.shape[0]
    nq = seq // tq
    k_off = SB_WIDTH // LANES
    return pl.pallas_call(
        functools.partial(_sb_kernel, tq=tq),
        grid=(batch, SB_WIDTH // LANES, nq),
        in_specs=[
            pl.BlockSpec((tq, LANES), lambda b, p, i: (b * nq + i, p)),
            pl.BlockSpec((seq, LANES), lambda b, p, i: (b, k_off + p)),
            pl.BlockSpec((None, None, LANES, seq), lambda b, p, i: (b, p, 0, 0)),
        ],
        out_specs=pl.BlockSpec((tq, LANES), lambda b, p, i: (b * nq + i, p)),
        out_shape=jax.ShapeDtypeStruct((n, SB_WIDTH), BF16),
        scratch_shapes=[pltpu.VMEM((LANES, 2 * tq), F32), pltpu.VMEM((1, 2 * tq), F32)],
        compiler_params=_params("arbitrary", "arbitrary", "arbitrary"),
        name="sb_attention",
    )(proj, proj, vt)


def _diff_kernel(lq1_ref, lk1_ref, lq2_ref, lk2_ref, g_ref, q_ref, k_ref, vt_ref, o_ref,
                 m_ref, acc_ref, s_ref, bmax_ref, *, tq, lambda_init):
    i = pl.program_id(2)
    q = q_ref[...]
    lane = lax.broadcasted_iota(jnp.int32, (tq, LANES), 1)
    low = lane < HEAD_DIM
    zero = jnp.zeros_like(q)
    q_maps = (jnp.where(low, q, zero), jnp.where(low, zero, q))
    ones = jnp.ones((ONES_ROWS, tq), BF16)

    m_ref[...] = jnp.full_like(m_ref, NEG)
    acc_ref[...] = jnp.zeros_like(acc_ref)

    def scores(j, diagonal):
        k = k_ref[pl.ds(pl.multiple_of(j * tq, tq), tq), :]
        for mp in range(2):
            s = _dot_nt(k, q_maps[mp])
            if diagonal:
                visible = ((lax.broadcasted_iota(jnp.int32, (tq, tq), 0) // CHUNK)
                           <= (lax.broadcasted_iota(jnp.int32, (tq, tq), 1) // CHUNK))
                s = jnp.where(visible, s, NEG)
            s_ref[mp] = s
            bmax_ref[mp] = jnp.max(s, axis=0, keepdims=True)

    def accumulate(j):
        vt = jnp.concatenate([vt_ref[:, pl.ds(pl.multiple_of(j * tq, tq), tq)], ones], axis=0)
        for mp in range(2):
            m_old = m_ref[mp]
            m_new = jnp.maximum(m_old, bmax_ref[mp])
            p = jnp.exp2(s_ref[mp] - m_new).astype(BF16)
            acc_ref[mp] = jnp.exp2(m_old - m_new) * acc_ref[mp] + _dot(vt, p)
            m_ref[mp] = m_new

    scores(i, True)

    def body(j, carry):
        accumulate(jnp.where(j == 0, i, j - 1))
        scores(j, False)
        return carry

    lax.fori_loop(0, i, body, 0)
    accumulate(jnp.where(i == 0, i, i - 1))

    lam = (jnp.exp(jnp.sum(lq1_ref[...] * lk1_ref[...], axis=1, keepdims=True))
           - jnp.exp(jnp.sum(lq2_ref[...] * lk2_ref[...], axis=1, keepdims=True))
           + lambda_init)
    acc1 = acc_ref[0]
    acc2 = acc_ref[1]
    out_t = (acc1[:LANES] * (1.0 / acc1[LANES:LANES + 1])
             - lam * (acc2[:LANES] * (1.0 / acc2[LANES:LANES + 1])))
    out = _rms(out_t.T, g_ref[...]) * (1.0 - lambda_init)
    o_ref[...] = out.astype(o_ref.dtype)


def _diff_attention(proj, vt, lq1, lk1, lq2, lk2, subln_g, batch, seq, tq, lambda_init):
    n = proj.shape[0]
    nq = seq // tq
    q_off = 2 * SB_WIDTH // LANES
    k_off = q_off + DIFF_WIDTH // LANES
    vec = _const_spec((1, HEAD_DIM))
    return pl.pallas_call(
        functools.partial(_diff_kernel, tq=tq, lambda_init=lambda_init),
        grid=(batch, DIFF_HEADS, nq),
        in_specs=[
            vec, vec, vec, vec,
            _const_spec((1, LANES)),
            pl.BlockSpec((tq, LANES), lambda b, h, i: (b * nq + i, q_off + h)),
            pl.BlockSpec((seq, LANES), lambda b, h, i: (b, k_off + h)),
            pl.BlockSpec((None, None, LANES, seq), lambda b, h, i: (b, h, 0, 0)),
        ],
        out_specs=pl.BlockSpec((tq, LANES), lambda b, h, i: (b * nq + i, h)),
        out_shape=jax.ShapeDtypeStruct((n, DIFF_WIDTH), BF16),
        scratch_shapes=[pltpu.VMEM((2, 1, tq), F32),
                        pltpu.VMEM((2, LANES + ONES_ROWS, tq), F32),
                        pltpu.VMEM((2, tq, tq), F32),
                        pltpu.VMEM((2, 1, tq), F32)],
        compiler_params=_params("arbitrary", "arbitrary", "arbitrary"),
        name="diff_attention",
    )(lq1, lk1, lq2, lk2, subln_g, proj, proj, vt)


def _out_ffn_kernel(x_ref, a_ref, b_ref, wo_ref, gf_ref, wgu_ref, wd_ref, gn_ref, o_ref,
                    acc_ref, *, final_norm):
    half = a_ref.shape[1]
    x1 = x_ref[...] + _dot(a_ref[...], wo_ref[:half, :]) + _dot(b_ref[...], wo_ref[half:, :])
    h = _rms(x1, gf_ref[...]).astype(BF16)
    acc_ref[...] = x1
    fc = wd_ref.shape[1]

    def body(c, carry):
        gu = _dot(h, wgu_ref[c])
        gate = gu[:, :fc]
        act = (gate * _sigmoid(gate) * gu[:, fc:]).astype(BF16)
        acc_ref[...] += _dot(act, wd_ref[c])
        return carry

    lax.fori_loop(0, wd_ref.shape[0], body, 0)
    y = acc_ref[...]
    if final_norm:
        y = _rms(y, gn_ref[...])
    o_ref[...] = y


def _out_ffn(x2, a, b, wo, gf, wgu, wd, gn, tm, final_norm):
    n = x2.shape[0]
    half = a.shape[1]
    nc, _, fc2 = wgu.shape
    return pl.pallas_call(
        functools.partial(_out_ffn_kernel, final_norm=final_norm),
        grid=(n // tm,),
        in_specs=[
            pl.BlockSpec((tm, D_MODEL), lambda i: (i, 0)),
            pl.BlockSpec((tm, half), lambda i: (i, 0)),
            pl.BlockSpec((tm, half), lambda i: (i, 0)),
            _const_spec((2 * half, D_MODEL)),
            _const_spec((1, D_MODEL)),
            _const_spec((nc, D_MODEL, fc2)),
            _const_spec((nc, fc2 // 2, D_MODEL)),
            _const_spec((1, D_MODEL)),
        ],
        out_specs=pl.BlockSpec((tm, D_MODEL), lambda i: (i, 0)),
        out_shape=jax.ShapeDtypeStruct((n, D_MODEL), F32),
        scratch_shapes=[pltpu.VMEM((tm, D_MODEL), F32)],
        compiler_params=_params("arbitrary"),
        name="out_proj_ffn",
    )(x2, a, b, wo, gf, wgu, wd, gn)


def _l1_mixer_kernel(x_ref, g_ref, w_ref, pw_ref, ps_ref, dw_ref, db_ref, cg_ref, cb_ref,
                     c_ref, d_ref, ext_p, ext_u):
    ti = pl.program_id(1)
    tm = x_ref.shape[0]

    @pl.when(ti == 0)
    def _():
        ext_p[:HALO, :] = jnp.zeros((HALO, POOL_WIDTH), F32)
        ext_u[:HALO, :] = jnp.zeros((HALO, CONV_WIDTH), F32)

    h = _rms(x_ref[...], g_ref[...]).astype(BF16)
    xp = _dot(h, w_ref[:, :POOL_WIDTH])
    xa = _dot(h, w_ref[:, POOL_WIDTH:POOL_WIDTH + CONV_WIDTH])
    xg = _dot(h, w_ref[:, POOL_WIDTH + CONV_WIDTH:])
    ext_p[HALO:, :] = xp
    ext_u[HALO:, :] = xa * _sigmoid(xg)

    pos1 = (ti * tm + 1 + lax.broadcasted_iota(jnp.int32, (tm, 1), 0)).astype(F32)
    for g, win in enumerate(POOL_WINDOWS):
        cols = slice(g * POOL_GROUP_DIM, (g + 1) * POOL_GROUP_DIM)
        total = ext_p[HALO:, cols]
        for back in range(1, win):
            total = total + ext_p[HALO - back:HALO - back + tm, cols]
        pooled = total / jnp.minimum(pos1, float(win)) - ext_p[HALO:, cols]
        mixed = _dot(pooled.astype(BF16), pw_ref[g])
        c_ref[:, cols] = (mixed * ps_ref[:, cols]).astype(c_ref.dtype)

    conv = jnp.zeros((tm, CONV_WIDTH), F32) + db_ref[...]
    for j in range(CONV_KERNEL):
        back = CONV_KERNEL - 1 - j
        conv = conv + ext_u[HALO - back:HALO - back + tm, :] * dw_ref[j:j + 1, :]
    mu = jnp.mean(conv, axis=-1, keepdims=True)
    cen = conv - mu
    var = jnp.mean(cen * cen, axis=-1, keepdims=True)
    y = cen * lax.rsqrt(var + EPS) * cg_ref[...] + cb_ref[...]
    d_ref[...] = (y * _sigmoid(y)).astype(d_ref.dtype)

    ext_p[:HALO, :] = ext_p[tm:, :]
    ext_u[:HALO, :] = ext_u[tm:, :]


def _l1_mixer(x2, g, w, pw, ps, dw, db, cg, cb, batch, seq, tm):
    n = x2.shape[0]
    nt = seq // tm
    row = lambda b, t: (b * nt + t, 0)
    return pl.pallas_call(
        _l1_mixer_kernel,
        grid=(batch, nt),
        in_specs=[
            pl.BlockSpec((tm, D_MODEL), row),
            _const_spec((1, D_MODEL)),
            _const_spec((D_MODEL, ODD_IN_WIDTH)),
            _const_spec((len(POOL_WINDOWS), POOL_GROUP_DIM, POOL_GROUP_DIM)),
            _const_spec((1, POOL_WIDTH)),
            _const_spec((CONV_KERNEL, CONV_WIDTH)),
            _const_spec((1, CONV_WIDTH)),
            _const_spec((1, CONV_WIDTH)),
            _const_spec((1, CONV_WIDTH)),
        ],
        out_specs=[pl.BlockSpec((tm, POOL_WIDTH), row), pl.BlockSpec((tm, CONV_WIDTH), row)],
        out_shape=[jax.ShapeDtypeStruct((n, POOL_WIDTH), BF16),
                   jax.ShapeDtypeStruct((n, CONV_WIDTH), BF16)],
        scratch_shapes=[pltpu.VMEM((HALO + tm, POOL_WIDTH), F32),
                        pltpu.VMEM((HALO + tm, CONV_WIDTH), F32)],
        compiler_params=_params("arbitrary", "arbitrary"),
        name="l1_mixer",
    )(x2, g, w, pw, ps, dw, db, cg, cb)


def _ffn_weights(w_gate, w_up, w_down):
    nc = D_FF // FF_CHUNK
    wg = w_gate.astype(BF16).reshape(D_MODEL, nc, FF_CHUNK)
    wu = w_up.astype(BF16).reshape(D_MODEL, nc, FF_CHUNK)
    wgu = jnp.concatenate([wg, wu], axis=2).transpose(1, 0, 2)
    wd = w_down.astype(BF16).reshape(nc, FF_CHUNK, D_MODEL)
    return wgu, wd


def _rope_tables(seq):
    half = HEAD_DIM // 2
    inv = jnp.power(ROPE_THETA, -jnp.arange(0, HEAD_DIM, 2, dtype=F32) / HEAD_DIM)
    ang = jnp.arange(seq, dtype=jnp.int32).astype(F32)[:, None] * inv[None, :]
    cos = jnp.cos(ang)
    sin = jnp.sin(ang)
    reps = LANES // HEAD_DIM
    return (jnp.tile(jnp.concatenate([cos, cos], axis=1), (1, reps)),
            jnp.tile(jnp.concatenate([-sin, sin], axis=1), (1, reps)))


def _tile(seq, want):
    return want if seq % want == 0 else seq


def kernel(x, mix_norm_0, w_in_0, lambda_q1_0, lambda_k1_0, lambda_q2_0, lambda_k2_0, subln_0, w_out_0, ffn_norm_0, w_gate_0, w_up_0, w_down_0, mix_norm_1, w_in_1, pool_w_1, pool_scale_1, dw_w_1, dw_b_1, conv_norm_g_1, conv_norm_b_1, w_out_1, ffn_norm_1, w_gate_1, w_up_1, w_down_1, final_norm):
    batch, seq, d = x.shape
    assert d == D_MODEL and seq % LANES == 0
    x2 = x.reshape(batch * seq, d)
    row = lambda v: v.reshape(1, -1).astype(F32)
    tm = _tile(seq, 512)

    cos_t, sin_t = _rope_tables(seq)
    proj, svt, dvt = _in_proj0(x2, row(mix_norm_0), w_in_0.astype(BF16), cos_t, sin_t,
                               batch, seq, tm)
    a_out = _sb_attention(proj, svt, batch, seq, _tile(seq, 256))
    lambda_init = 0.8 - 0.6 * math.exp(-0.3 * 0)
    b_out = _diff_attention(proj, dvt, row(lambda_q1_0), row(lambda_k1_0), row(lambda_q2_0),
                            row(lambda_k2_0), row(subln_0), batch, seq, _tile(seq, 512),
                            lambda_init)
    wgu0, wd0 = _ffn_weights(w_gate_0, w_up_0, w_down_0)
    x2 = _out_ffn(x2, a_out, b_out, w_out_0.astype(BF16), row(ffn_norm_0), wgu0, wd0,
                  row(final_norm), tm, final_norm=False)

    c_out, d_out = _l1_mixer(x2, row(mix_norm_1), w_in_1.astype(BF16), pool_w_1.astype(BF16),
                             row(pool_scale_1), dw_w_1.astype(F32), row(dw_b_1),
                             row(conv_norm_g_1), row(conv_norm_b_1), batch, seq, tm)
    wgu1, wd1 = _ffn_weights(w_gate_1, w_up_1, w_down_1)
    out = _out_ffn(x2, c_out, d_out, w_out_1.astype(BF16), row(ffn_norm_1), wgu1, wd1,
                   row(final_norm), tm, final_norm=True)
    return out.reshape(batch, seq, d)
```

```python
import functools
import math

import jax
import jax.numpy as jnp
from jax import lax
from jax.experimental import pallas as pl
from jax.experimental.pallas import tpu as pltpu

F32 = jnp.float32
BF16 = jnp.bfloat16

D_MODEL = 1024
HEAD_DIM = 64
LANES = 128
MXU_N = 256
SB_HEADS = 8
DIFF_HEADS = 4
SB_WIDTH = SB_HEADS * HEAD_DIM
DIFF_WIDTH = DIFF_HEADS * 2 * HEAD_DIM
EVEN_IN_WIDTH = 3 * SB_WIDTH + 3 * DIFF_WIDTH
CHUNK = 64
POOL_WINDOWS = (2, 4, 8, 16)
POOL_WIDTH = 512
POOL_GROUP_DIM = 128
CONV_WIDTH = 512
CONV_KERNEL = 31
ODD_IN_WIDTH = POOL_WIDTH + 2 * CONV_WIDTH
D_FF = 2816
FF_CHUNK = 256
ROPE_THETA = 10000.0
EPS = 1e-6
NEG = -1e30
SB_SCALE = HEAD_DIM ** -0.5
LOG2E = math.log2(math.e)
ONES_ROWS = 16
HALO = 32
QK_WIDTH = 2 * SB_WIDTH + 2 * DIFF_WIDTH
SB_LOG2_ZERO = -150.0 * math.log2(math.e)
VMEM_LIMIT = 56 * 1024 * 1024


def _rms(x, g):
    return x * lax.rsqrt(jnp.mean(x * x, axis=-1, keepdims=True) + EPS) * g


def _sigmoid(x):
    return 1.0 / (1.0 + jnp.exp(-x))


def _dot(a, b):
    return jnp.dot(a, b, preferred_element_type=F32)


def _dot_nt(a, b):
    return lax.dot_general(a, b, (((1,), (1,)), ((), ())), preferred_element_type=F32)


def _params(*sem):
    return pltpu.CompilerParams(dimension_semantics=sem, vmem_limit_bytes=VMEM_LIMIT)


def _const_spec(shape):
    nd = len(shape)
    return pl.BlockSpec(shape, lambda *_: (0,) * nd)


def _in_proj0_kernel(x_ref, g_ref, w_ref, cos_ref, sin_ref, o_ref, svt_ref, dvt_ref):
    h = _rms(x_ref[...], g_ref[...]).astype(BF16)
    tm = h.shape[0]
    lane = lax.broadcasted_iota(jnp.int32, (tm, LANES), 1)
    first_half = (lane % HEAD_DIM) < (HEAD_DIM // 2)
    cos = cos_ref[...]
    sin = sin_ref[...]
    seg = SB_WIDTH
    q_scale = SB_SCALE * LOG2E
    out_col = {0: 0, 1: 1, 3: 2, 4: 3}
    for c in range(EVEN_IN_WIDTH // seg):
        y = _dot(h, w_ref[:, c * seg:(c + 1) * seg])
        for s in range(seg // LANES):
            t = y[:, s * LANES:(s + 1) * LANES]
            if c in (2, 5):
                (svt_ref if c == 2 else dvt_ref)[0, s] = t.T.astype(BF16)
                continue
            if c in (3, 4):
                swapped = jnp.where(first_half,
                                    pltpu.roll(t, LANES - HEAD_DIM // 2, 1),
                                    pltpu.roll(t, HEAD_DIM // 2, 1))
                t = t * cos + swapped * sin
            if c in (0, 3):
                t = t * q_scale
            col = out_col[c] * seg + s * LANES
            o_ref[:, col:col + LANES] = t.astype(BF16)


def _in_proj0(x2, g, w, cos_t, sin_t, batch, seq, tm):
    n = x2.shape[0]
    per_seq = seq // tm
    vt_spec = pl.BlockSpec((1, SB_WIDTH // LANES, LANES, tm),
                           lambda i: (i // per_seq, 0, 0, i % per_seq))
    vt_shape = jax.ShapeDtypeStruct((batch, SB_WIDTH // LANES, LANES, seq), BF16)
    return pl.pallas_call(
        _in_proj0_kernel,
        grid=(n // tm,),
        in_specs=[
            pl.BlockSpec((tm, D_MODEL), lambda i: (i, 0)),
            _const_spec((1, D_MODEL)),
            _const_spec((D_MODEL, EVEN_IN_WIDTH)),
            pl.BlockSpec((tm, LANES), lambda i: (i % per_seq, 0)),
            pl.BlockSpec((tm, LANES), lambda i: (i % per_seq, 0)),
        ],
        out_specs=[pl.BlockSpec((tm, QK_WIDTH), lambda i: (i, 0)), vt_spec, vt_spec],
        out_shape=[jax.ShapeDtypeStruct((n, QK_WIDTH), BF16), vt_shape, vt_shape],
        compiler_params=_params("arbitrary"),
        name="l0_in_proj",
    )(x2, g, w, cos_t, sin_t)


def _sb_kernel(q_ref, k_ref, vt_ref, o_ref, acc_ref, rest_ref, *, tq):
    i = pl.program_id(2)
    q = q_ref[...]
    lane = lax.broadcasted_iota(jnp.int32, (tq, LANES), 1)
    low = lane < HEAD_DIM
    zero = jnp.zeros_like(q)
    q2 = jnp.concatenate([jnp.where(low, q, zero), jnp.where(low, zero, q)], axis=0)
    tri_row = lax.broadcasted_iota(jnp.int32, (tq + ONES_ROWS, tq), 0)
    tri_col = lax.broadcasted_iota(jnp.int32, (tq + ONES_ROWS, tq), 1)
    tri = jnp.logical_or(tri_col > tri_row, tri_row >= tq).astype(BF16)

    def sweep(first, nblk, diagonal):
        nk = nblk * tq
        start = pl.multiple_of(first * tq, tq)
        z = _dot_nt(k_ref[pl.ds(start, nk), :], q2)
        log_1m = jnp.minimum(-z, 0.0) - jnp.log2(1.0 + jnp.exp2(-jnp.abs(z)))
        if diagonal:
            key = lax.broadcasted_iota(jnp.int32, (nk, 2 * tq), 0) - (nk - tq)
            qry = jnp.bitwise_and(lax.broadcasted_iota(jnp.int32, (nk, 2 * tq), 1), tq - 1)
            strict = key < qry
            log_1m = jnp.where(strict, log_1m, 0.0)
        hi = log_1m.astype(BF16)
        lo = (log_1m - hi.astype(F32)).astype(BF16)
        rest = rest_ref[...]
        log_w = []
        for b in reversed(range(nblk)):
            rows = slice(b * tq, (b + 1) * tq)
            after = _dot(tri, hi[rows]) + _dot(tri, lo[rows])
            log_w.append(z[rows] + log_1m[rows] + after[:tq] + rest)
            rest = rest + after[tq:tq + 1]
        w = jnp.exp2(jnp.concatenate(log_w[::-1], axis=0))
        if diagonal:
            w = jnp.where(strict, w, 0.0)
        acc_ref[...] += _dot(vt_ref[:, pl.ds(start, nk)], w.astype(BF16))
        rest_ref[...] = rest

    acc_ref[...] = jnp.zeros_like(acc_ref)
    rest_ref[...] = jnp.zeros_like(rest_ref)

    @pl.when(i == 0)
    def _():
        sweep(0, 1, True)

    @pl.when(i > 0)
    def _():
        sweep(i - 1, 2, True)

    def cond(st):
        j, worst = st
        return jnp.logical_and(j >= 0, worst > SB_LOG2_ZERO)

    def body(st):
        j, _ = st
        sweep(j, 1, False)
        return j - 1, jnp.max(rest_ref[...])

    lax.while_loop(cond, body, (i - 2, jnp.max(rest_ref[...])))
    head0_rows = lax.broadcasted_iota(jnp.int32, (LANES, tq), 0) < HEAD_DIM
    acc = acc_ref[...]
    o_ref[...] = jnp.where(head0_rows, acc[:, :tq], acc[:, tq:]).T.astype(o_ref.dtype)


def _sb_attention(proj, vt, batch, seq, tq):
    n = proj.shape[0]
    nq = seq // tq
    k_off = SB_WIDTH // LANES
    return pl.pallas_call(
        functools.partial(_sb_kernel, tq=tq),
        grid=(batch, SB_WIDTH // LANES, nq),
        in_specs=[
            pl.BlockSpec((tq, LANES), lambda b, p, i: (b * nq + i, p)),
            pl.BlockSpec((seq, LANES), lambda b, p, i: (b, k_off + p)),
            pl.BlockSpec((None, None, LANES, seq), lambda b, p, i: (b, p, 0, 0)),
        ],
        out_specs=pl.BlockSpec((tq, LANES), lambda b, p, i: (b * nq + i, p)),
        out_shape=jax.ShapeDtypeStruct((n, SB_WIDTH), BF16),
        scratch_shapes=[pltpu.VMEM((LANES, 2 * tq), F32), pltpu.VMEM((1, 2 * tq), F32)],
        compiler_params=_params("arbitrary", "arbitrary", "arbitrary"),
        name="sb_attention",
    )(proj, proj, vt)


def _diff_kernel(lq1_ref, lk1_ref, lq2_ref, lk2_ref, g_ref, q_ref, k_ref, vt_ref, o_ref,
                 m_ref, acc_ref, s_ref, bmax_ref, *, tq, lambda_init):
    i = pl.program_id(2)
    q = q_ref[...]
    lane = lax.broadcasted_iota(jnp.int32, (tq, LANES), 1)
    low = lane < HEAD_DIM
    zero = jnp.zeros_like(q)
    q_maps = (jnp.where(low, q, zero), jnp.where(low, zero, q))
    ones = jnp.ones((ONES_ROWS, tq), BF16)

    m_ref[...] = jnp.full_like(m_ref, NEG)
    acc_ref[...] = jnp.zeros_like(acc_ref)

    def scores(j, buf, diagonal=False):
        k = k_ref[pl.ds(pl.multiple_of(j * tq, tq), tq), :]
        for mp in range(2):
            s = _dot_nt(k, q_maps[mp])
            if diagonal:
                visible = ((lax.broadcasted_iota(jnp.int32, (tq, tq), 0) // CHUNK)
                           <= (lax.broadcasted_iota(jnp.int32, (tq, tq), 1) // CHUNK))
                s = jnp.where(visible, s, NEG)
            s_ref[buf, mp] = s
            bmax_ref[buf, mp] = jnp.max(s, axis=0, keepdims=True)

    def accumulate(j, buf):
        vt = jnp.concatenate([vt_ref[:, pl.ds(pl.multiple_of(j * tq, tq), tq)], ones], axis=0)
        for mp in range(2):
            m_old = m_ref[mp]
            m_new = jnp.maximum(m_old, bmax_ref[buf, mp])
            p = jnp.exp2(s_ref[buf, mp] - m_new).astype(BF16)
            acc_ref[mp] = jnp.exp2(m_old - m_new) * acc_ref[mp] + _dot(vt, p)
            m_ref[mp] = m_new

    scores(i, 0, diagonal=True)
    pairs = i // 2

    def body(jj, carry):
        scores(2 * jj, 1)
        accumulate(jnp.where(jj == 0, i, 2 * jj - 1), 0)
        scores(2 * jj + 1, 0)
        accumulate(2 * jj, 1)
        return carry

    lax.fori_loop(0, pairs, body, 0)
    held = jnp.where(pairs == 0, i, 2 * pairs - 1)

    @pl.when(i % 2 == 1)
    def _():
        scores(i - 1, 1)
        accumulate(held, 0)
        accumulate(i - 1, 1)

    @pl.when(i % 2 == 0)
    def _():
        accumulate(held, 0)

    lam = (jnp.exp(jnp.sum(lq1_ref[...] * lk1_ref[...], axis=1, keepdims=True))
           - jnp.exp(jnp.sum(lq2_ref[...] * lk2_ref[...], axis=1, keepdims=True))
           + lambda_init)
    acc1 = acc_ref[0]
    acc2 = acc_ref[1]
    out_t = (acc1[:LANES] * (1.0 / acc1[LANES:LANES + 1])
             - lam * (acc2[:LANES] * (1.0 / acc2[LANES:LANES + 1])))
    out = _rms(out_t.T, g_ref[...]) * (1.0 - lambda_init)
    o_ref[...] = out.astype(o_ref.dtype)


def _diff_attention(proj, vt, lq1, lk1, lq2, lk2, subln_g, batch, seq, tq, lambda_init):
    n = proj.shape[0]
    nq = seq // tq
    q_off = 2 * SB_WIDTH // LANES
    k_off = q_off + DIFF_WIDTH // LANES
    vec = _const_spec((1, HEAD_DIM))
    return pl.pallas_call(
        functools.partial(_diff_kernel, tq=tq, lambda_init=lambda_init),
        grid=(batch, DIFF_HEADS, nq),
        in_specs=[
            vec, vec, vec, vec,
            _const_spec((1, LANES)),
            pl.BlockSpec((tq, LANES), lambda b, h, i: (b * nq + i, q_off + h)),
            pl.BlockSpec((seq, LANES), lambda b, h, i: (b, k_off + h)),
            pl.BlockSpec((None, None, LANES, seq), lambda b, h, i: (b, h, 0, 0)),
        ],
        out_specs=pl.BlockSpec((tq, LANES), lambda b, h, i: (b * nq + i, h)),
        out_shape=jax.ShapeDtypeStruct((n, DIFF_WIDTH), BF16),
        scratch_shapes=[pltpu.VMEM((2, 1, tq), F32),
                        pltpu.VMEM((2, LANES + ONES_ROWS, tq), F32),
                        pltpu.VMEM((2, 2, tq, tq), F32),
                        pltpu.VMEM((2, 2, 1, tq), F32)],
        compiler_params=_params("arbitrary", "arbitrary", "arbitrary"),
        name="diff_attention",
    )(lq1, lk1, lq2, lk2, subln_g, proj, proj, vt)


def _out_ffn_kernel(x_ref, a_ref, b_ref, wo_ref, gf_ref, wgu_ref, wd_ref, gn_ref, o_ref,
                    acc_ref, *, final_norm):
    half = a_ref.shape[1]
    x1 = x_ref[...] + _dot(a_ref[...], wo_ref[:half, :]) + _dot(b_ref[...], wo_ref[half:, :])
    h = _rms(x1, gf_ref[...]).astype(BF16)
    acc_ref[...] = x1
    fc = wd_ref.shape[1]

    def body(c, carry):
        gu = _dot(h, wgu_ref[c])
        gate = gu[:, :fc]
        act = (gate * _sigmoid(gate) * gu[:, fc:]).astype(BF16)
        acc_ref[...] += _dot(act, wd_ref[c])
        return carry

    lax.fori_loop(0, wd_ref.shape[0], body, 0)
    y = acc_ref[...]
    if final_norm:
        y = _rms(y, gn_ref[...])
    o_ref[...] = y


def _out_ffn(x2, a, b, wo, gf, wgu, wd, gn, tm, final_norm):
    n = x2.shape[0]
    half = a.shape[1]
    nc, _, fc2 = wgu.shape
    return pl.pallas_call(
        functools.partial(_out_ffn_kernel, final_norm=final_norm),
        grid=(n // tm,),
        in_specs=[
            pl.BlockSpec((tm, D_MODEL), lambda i: (i, 0)),
            pl.BlockSpec((tm, half), lambda i: (i, 0)),
            pl.BlockSpec((tm, half), lambda i: (i, 0)),
            _const_spec((2 * half, D_MODEL)),
            _const_spec((1, D_MODEL)),
            _const_spec((nc, D_MODEL, fc2)),
            _const_spec((nc, fc2 // 2, D_MODEL)),
            _const_spec((1, D_MODEL)),
        ],
        out_specs=pl.BlockSpec((tm, D_MODEL), lambda i: (i, 0)),
        out_shape=jax.ShapeDtypeStruct((n, D_MODEL), F32),
        scratch_shapes=[pltpu.VMEM((tm, D_MODEL), F32)],
        compiler_params=_params("arbitrary"),
        name="out_proj_ffn",
    )(x2, a, b, wo, gf, wgu, wd, gn)


def _l1_mixer_kernel(x_ref, g_ref, w_ref, pw_ref, ps_ref, dw_ref, db_ref, cg_ref, cb_ref,
                     c_ref, d_ref, ext_p, ext_u):
    ti = pl.program_id(1)
    tm = x_ref.shape[0]

    @pl.when(ti == 0)
    def _():
        ext_p[:HALO, :] = jnp.zeros((HALO, POOL_WIDTH), F32)
        ext_u[:HALO, :] = jnp.zeros((HALO, CONV_WIDTH), F32)

    h = _rms(x_ref[...], g_ref[...]).astype(BF16)
    xp = _dot(h, w_ref[:, :POOL_WIDTH])
    xa = _dot(h, w_ref[:, POOL_WIDTH:POOL_WIDTH + CONV_WIDTH])
    xg = _dot(h, w_ref[:, POOL_WIDTH + CONV_WIDTH:])
    ext_p[HALO:, :] = xp
    ext_u[HALO:, :] = xa * _sigmoid(xg)

    pos1 = (ti * tm + 1 + lax.broadcasted_iota(jnp.int32, (tm, 1), 0)).astype(F32)
    for g, win in enumerate(POOL_WINDOWS):
        cols = slice(g * POOL_GROUP_DIM, (g + 1) * POOL_GROUP_DIM)
        total = ext_p[HALO:, cols]
        for back in range(1, win):
            total = total + ext_p[HALO - back:HALO - back + tm, cols]
        pooled = total / jnp.minimum(pos1, float(win)) - ext_p[HALO:, cols]
        mixed = _dot(pooled.astype(BF16), pw_ref[g])
        c_ref[:, cols] = (mixed * ps_ref[:, cols]).astype(c_ref.dtype)

    conv = jnp.zeros((tm, CONV_WIDTH), F32) + db_ref[...]
    for j in range(CONV_KERNEL):
        back = CONV_KERNEL - 1 - j
        conv = conv + ext_u[HALO - back:HALO - back + tm, :] * dw_ref[j:j + 1, :]
    mu = jnp.mean(conv, axis=-1, keepdims=True)
    cen = conv - mu
    var = jnp.mean(cen * cen, axis=-1, keepdims=True)
    y = cen * lax.rsqrt(var + EPS) * cg_ref[...] + cb_ref[...]
    d_ref[...] = (y * _sigmoid(y)).astype(d_ref.dtype)

    ext_p[:HALO, :] = ext_p[tm:, :]
    ext_u[:HALO, :] = ext_u[tm:, :]


def _l1_mixer(x2, g, w, pw, ps, dw, db, cg, cb, batch, seq, tm):
    n = x2.shape[0]
    nt = seq // tm
    row = lambda b, t: (b * nt + t, 0)
    return pl.pallas_call(
        _l1_mixer_kernel,
        grid=(batch, nt),
        in_specs=[
            pl.BlockSpec((tm, D_MODEL), row),
            _const_spec((1, D_MODEL)),
            _const_spec((D_MODEL, ODD_IN_WIDTH)),
            _const_spec((len(POOL_WINDOWS), POOL_GROUP_DIM, POOL_GROUP_DIM)),
            _const_spec((1, POOL_WIDTH)),
            _const_spec((CONV_KERNEL, CONV_WIDTH)),
            _const_spec((1, CONV_WIDTH)),
            _const_spec((1, CONV_WIDTH)),
            _const_spec((1, CONV_WIDTH)),
        ],
        out_specs=[pl.BlockSpec((tm, POOL_WIDTH), row), pl.BlockSpec((tm, CONV_WIDTH), row)],
        out_shape=[jax.ShapeDtypeStruct((n, POOL_WIDTH), BF16),
                   jax.ShapeDtypeStruct((n, CONV_WIDTH), BF16)],
        scratch_shapes=[pltpu.VMEM((HALO + tm, POOL_WIDTH), F32),
                        pltpu.VMEM((HALO + tm, CONV_WIDTH), F32)],
        compiler_params=_params("arbitrary", "arbitrary"),
        name="l1_mixer",
    )(x2, g, w, pw, ps, dw, db, cg, cb)


def _ffn_weights(w_gate, w_up, w_down):
    nc = D_FF // FF_CHUNK
    wg = w_gate.astype(BF16).reshape(D_MODEL, nc, FF_CHUNK)
    wu = w_up.astype(BF16).reshape(D_MODEL, nc, FF_CHUNK)
    wgu = jnp.concatenate([wg, wu], axis=2).transpose(1, 0, 2)
    wd = w_down.astype(BF16).reshape(nc, FF_CHUNK, D_MODEL)
    return wgu, wd


def _rope_tables(seq):
    half = HEAD_DIM // 2
    inv = jnp.power(ROPE_THETA, -jnp.arange(0, HEAD_DIM, 2, dtype=F32) / HEAD_DIM)
    ang = jnp.arange(seq, dtype=jnp.int32).astype(F32)[:, None] * inv[None, :]
    cos = jnp.cos(ang)
    sin = jnp.sin(ang)
    reps = LANES // HEAD_DIM
    return (jnp.tile(jnp.concatenate([cos, cos], axis=1), (1, reps)),
            jnp.tile(jnp.concatenate([-sin, sin], axis=1), (1, reps)))


def _tile(seq, want):
    return want if seq % want == 0 else seq


def kernel(x, mix_norm_0, w_in_0, lambda_q1_0, lambda_k1_0, lambda_q2_0, lambda_k2_0, subln_0, w_out_0, ffn_norm_0, w_gate_0, w_up_0, w_down_0, mix_norm_1, w_in_1, pool_w_1, pool_scale_1, dw_w_1, dw_b_1, conv_norm_g_1, conv_norm_b_1, w_out_1, ffn_norm_1, w_gate_1, w_up_1, w_down_1, final_norm):
    batch, seq, d = x.shape
    assert d == D_MODEL and seq % LANES == 0
    x2 = x.reshape(batch * seq, d)
    row = lambda v: v.reshape(1, -1).astype(F32)
    tm = _tile(seq, 512)

    cos_t, sin_t = _rope_tables(seq)
    proj, svt, dvt = _in_proj0(x2, row(mix_norm_0), w_in_0.astype(BF16), cos_t, sin_t,
                               batch, seq, tm)
    a_out = _sb_attention(proj, svt, batch, seq, _tile(seq, 256))
    lambda_init = 0.8 - 0.6 * math.exp(-0.3 * 0)
    b_out = _diff_attention(proj, dvt, row(lambda_q1_0), row(lambda_k1_0), row(lambda_q2_0),
                            row(lambda_k2_0), row(subln_0), batch, seq, _tile(seq, 512),
                            lambda_init)
    wgu0, wd0 = _ffn_weights(w_gate_0, w_up_0, w_down_0)
    x2 = _out_ffn(x2, a_out, b_out, w_out_0.astype(BF16), row(ffn_norm_0), wgu0, wd0,
                  row(final_norm), tm, final_norm=False)

    c_out, d_out = _l1_mixer(x2, row(mix_norm_1), w_in_1.astype(BF16), pool_w_1.astype(BF16),
                             row(pool_scale_1), dw_w_1.astype(F32), row(dw_b_1),
                             row(conv_norm_g_1), row(conv_norm_b_1), batch, seq, tm)
    wgu1, wd1 = _ffn_weights(w_gate_1, w_up_1, w_down_1)
    out = _out_ffn(x2, c_out, d_out, w_out_1.astype(BF16), row(ffn_norm_1), wgu1, wd1,
                   row(final_norm), tm, final_norm=True)
    return out.reshape(batch, seq, d)
```

```python
import functools
import math

import jax
import jax.numpy as jnp
from jax import lax
from jax.experimental import pallas as pl
from jax.experimental.pallas import tpu as pltpu

F32 = jnp.float32
BF16 = jnp.bfloat16

D_MODEL = 1024
HEAD_DIM = 64
LANES = 128
SUBLANES = 8
CONV_ROWS = 64
SB_HEADS = 8
DIFF_HEADS = 4
SB_WIDTH = SB_HEADS * HEAD_DIM
DIFF_WIDTH = DIFF_HEADS * 2 * HEAD_DIM
EVEN_IN_WIDTH = 3 * SB_WIDTH + 3 * DIFF_WIDTH
CHUNK = 64
POOL_WINDOWS = (2, 4, 8, 16)
POOL_WIDTH = 512
POOL_GROUP_DIM = 128
CONV_WIDTH = 512
CONV_KERNEL = 31
ODD_IN_WIDTH = POOL_WIDTH + 2 * CONV_WIDTH
D_FF = 2816
FF_CHUNK = 256
ROPE_THETA = 10000.0
EPS = 1e-6
NEG = -1e30
SB_SCALE = HEAD_DIM ** -0.5
LOG2E = math.log2(math.e)
ONES_ROWS = 16
HALO = 32
QK_WIDTH = 2 * SB_WIDTH + 2 * DIFF_WIDTH
SB_LOG2_ZERO = -150.0 * math.log2(math.e)
VMEM_LIMIT = 56 * 1024 * 1024


def _rms(x, g):
    return x * lax.rsqrt(jnp.mean(x * x, axis=-1, keepdims=True) + EPS) * g


def _sigmoid(x):
    return 1.0 / (1.0 + jnp.exp(-x))


def _dot(a, b):
    return jnp.dot(a, b, preferred_element_type=F32)


def _dot_nt(a, b):
    return lax.dot_general(a, b, (((1,), (1,)), ((), ())), preferred_element_type=F32)


def _params(*sem):
    return pltpu.CompilerParams(dimension_semantics=sem, vmem_limit_bytes=VMEM_LIMIT)


def _const_spec(shape):
    nd = len(shape)
    return pl.BlockSpec(shape, lambda *_: (0,) * nd)


def _in_proj0_kernel(x_ref, g_ref, w_ref, cos_ref, sin_ref, o_ref, svt_ref, dvt_ref):
    h = _rms(x_ref[...], g_ref[...]).astype(BF16)
    tm = h.shape[0]
    lane = lax.broadcasted_iota(jnp.int32, (tm, LANES), 1)
    first_half = (lane % HEAD_DIM) < (HEAD_DIM // 2)
    cos = cos_ref[...]
    sin = sin_ref[...]
    seg = SB_WIDTH
    q_scale = SB_SCALE * LOG2E
    out_col = {0: 0, 1: 1, 3: 2, 4: 3}
    for c in range(EVEN_IN_WIDTH // seg):
        y = _dot(h, w_ref[:, c * seg:(c + 1) * seg])
        for s in range(seg // LANES):
            t = y[:, s * LANES:(s + 1) * LANES]
            if c in (2, 5):
                (svt_ref if c == 2 else dvt_ref)[0, s] = t.T.astype(BF16)
                continue
            if c in (3, 4):
                swapped = jnp.where(first_half,
                                    pltpu.roll(t, LANES - HEAD_DIM // 2, 1),
                                    pltpu.roll(t, HEAD_DIM // 2, 1))
                t = t * cos + swapped * sin
            if c in (0, 3):
                t = t * q_scale
            col = out_col[c] * seg + s * LANES
            o_ref[:, col:col + LANES] = t.astype(BF16)


def _in_proj0(x2, g, w, cos_t, sin_t, batch, seq, tm):
    n = x2.shape[0]
    per_seq = seq // tm
    vt_spec = pl.BlockSpec((1, SB_WIDTH // LANES, LANES, tm),
                           lambda i: (i // per_seq, 0, 0, i % per_seq))
    vt_shape = jax.ShapeDtypeStruct((batch, SB_WIDTH // LANES, LANES, seq), BF16)
    return pl.pallas_call(
        _in_proj0_kernel,
        grid=(n // tm,),
        in_specs=[
            pl.BlockSpec((tm, D_MODEL), lambda i: (i, 0)),
            _const_spec((1, D_MODEL)),
            _const_spec((D_MODEL, EVEN_IN_WIDTH)),
            pl.BlockSpec((tm, LANES), lambda i: (i % per_seq, 0)),
            pl.BlockSpec((tm, LANES), lambda i: (i % per_seq, 0)),
        ],
        out_specs=[pl.BlockSpec((tm, QK_WIDTH), lambda i: (i, 0)), vt_spec, vt_spec],
        out_shape=[jax.ShapeDtypeStruct((n, QK_WIDTH), BF16), vt_shape, vt_shape],
        compiler_params=_params("arbitrary"),
        name="l0_in_proj",
    )(x2, g, w, cos_t, sin_t)


def _sb_kernel(q_ref, k_ref, vt_ref, o_ref, acc_ref, rest_ref, *, tq):
    i = pl.program_id(2)
    q = q_ref[...]
    lane = lax.broadcasted_iota(jnp.int32, (tq, LANES), 1)
    low = lane < HEAD_DIM
    zero = jnp.zeros_like(q)
    q2 = jnp.concatenate([jnp.where(low, q, zero), jnp.where(low, zero, q)], axis=0)
    tri_row = lax.broadcasted_iota(jnp.int32, (tq + ONES_ROWS, tq), 0)
    tri_col = lax.broadcasted_iota(jnp.int32, (tq + ONES_ROWS, tq), 1)
    tri = jnp.logical_or(tri_col > tri_row, tri_row >= tq).astype(BF16)

    def sweep(first, nblk, diagonal):
        nk = nblk * tq
        start = pl.multiple_of(first * tq, tq)
        z = _dot_nt(k_ref[pl.ds(start, nk), :], q2)
        log_1m = jnp.minimum(-z, 0.0) - jnp.log2(1.0 + jnp.exp2(-jnp.abs(z)))
        if diagonal:
            key = lax.broadcasted_iota(jnp.int32, (nk, 2 * tq), 0) - (nk - tq)
            qry = jnp.bitwise_and(lax.broadcasted_iota(jnp.int32, (nk, 2 * tq), 1), tq - 1)
            strict = key < qry
            log_1m = jnp.where(strict, log_1m, 0.0)
        hi = log_1m.astype(BF16)
        lo = (log_1m - hi.astype(F32)).astype(BF16)
        rest = rest_ref[...]
        log_w = []
        for b in reversed(range(nblk)):
            rows = slice(b * tq, (b + 1) * tq)
            after = _dot(tri, hi[rows]) + _dot(tri, lo[rows])
            log_w.append(z[rows] + log_1m[rows] + after[:tq] + rest)
            rest = rest + after[tq:tq + 1]
        w = jnp.exp2(jnp.concatenate(log_w[::-1], axis=0))
        if diagonal:
            w = jnp.where(strict, w, 0.0)
        acc_ref[...] += _dot(vt_ref[:, pl.ds(start, nk)], w.astype(BF16))
        rest_ref[...] = rest

    acc_ref[...] = jnp.zeros_like(acc_ref)
    rest_ref[...] = jnp.zeros_like(rest_ref)

    @pl.when(i == 0)
    def _():
        sweep(0, 1, True)

    @pl.when(i > 0)
    def _():
        sweep(i - 1, 2, True)

    def cond(st):
        j, worst = st
        return jnp.logical_and(j >= 0, worst > SB_LOG2_ZERO)

    def body(st):
        j, _ = st
        sweep(j, 1, False)
        return j - 1, jnp.max(rest_ref[...])

    lax.while_loop(cond, body, (i - 2, jnp.max(rest_ref[...])))
    head0_rows = lax.broadcasted_iota(jnp.int32, (LANES, tq), 0) < HEAD_DIM
    acc = acc_ref[...]
    o_ref[...] = jnp.where(head0_rows, acc[:, :tq], acc[:, tq:]).T.astype(o_ref.dtype)


def _sb_attention(proj, vt, batch, seq, tq):
    n = proj.shape[0]
    nq = seq // tq
    k_off = SB_WIDTH // LANES
    return pl.pallas_call(
        functools.partial(_sb_kernel, tq=tq),
        grid=(batch, SB_WIDTH // LANES, nq),
        in_specs=[
            pl.BlockSpec((tq, LANES), lambda b, p, i: (b * nq + i, p)),
            pl.BlockSpec((seq, LANES), lambda b, p, i: (b, k_off + p)),
            pl.BlockSpec((None, None, LANES, seq), lambda b, p, i: (b, p, 0, 0)),
        ],
        out_specs=pl.BlockSpec((tq, LANES), lambda b, p, i: (b * nq + i, p)),
        out_shape=jax.ShapeDtypeStruct((n, SB_WIDTH), BF16),
        scratch_shapes=[pltpu.VMEM((LANES, 2 * tq), F32), pltpu.VMEM((1, 2 * tq), F32)],
        compiler_params=_params("arbitrary", "arbitrary", "arbitrary"),
        name="sb_attention",
    )(proj, proj, vt)


def _diff_kernel(lq1_ref, lk1_ref, lq2_ref, lk2_ref, g_ref, q_ref, k_ref, vt_ref, o_ref,
                 m_ref, acc_ref, s_ref, bmax_ref, *, tq, lambda_init):
    i = pl.program_id(2)
    q = q_ref[...]
    lane = lax.broadcasted_iota(jnp.int32, (tq, LANES), 1)
    low = lane < HEAD_DIM
    zero = jnp.zeros_like(q)
    q_maps = (jnp.where(low, q, zero), jnp.where(low, zero, q))
    ones = jnp.ones((ONES_ROWS, tq), BF16)

    m_ref[...] = jnp.full_like(m_ref, NEG)
    acc_ref[...] = jnp.zeros_like(acc_ref)

    def scores(j, buf, diagonal=False):
        k = k_ref[pl.ds(pl.multiple_of(j * tq, tq), tq), :]
        for mp in range(2):
            s = _dot_nt(k, q_maps[mp])
            if diagonal:
                visible = ((lax.broadcasted_iota(jnp.int32, (tq, tq), 0) // CHUNK)
                           <= (lax.broadcasted_iota(jnp.int32, (tq, tq), 1) // CHUNK))
                s = jnp.where(visible, s, NEG)
            s_ref[buf, mp] = s
            bmax_ref[buf, mp] = jnp.max(s, axis=0, keepdims=True)

    def accumulate(j, buf):
        vt = jnp.concatenate([vt_ref[:, pl.ds(pl.multiple_of(j * tq, tq), tq)], ones], axis=0)
        for mp in range(2):
            m_old = m_ref[mp]
            m_new = jnp.maximum(m_old, bmax_ref[buf, mp])
            p = jnp.exp2(s_ref[buf, mp] - m_new).astype(BF16)
            acc_ref[mp] = jnp.exp2(m_old - m_new) * acc_ref[mp] + _dot(vt, p)
            m_ref[mp] = m_new

    scores(i, 0, diagonal=True)
    pairs = i // 2

    def body(jj, carry):
        scores(2 * jj, 1)
        accumulate(jnp.where(jj == 0, i, 2 * jj - 1), 0)
        scores(2 * jj + 1, 0)
        accumulate(2 * jj, 1)
        return carry

    lax.fori_loop(0, pairs, body, 0)
    held = jnp.where(pairs == 0, i, 2 * pairs - 1)

    @pl.when(i % 2 == 1)
    def _():
        scores(i - 1, 1)
        accumulate(held, 0)
        accumulate(i - 1, 1)

    @pl.when(i % 2 == 0)
    def _():
        accumulate(held, 0)

    lam = (jnp.exp(jnp.sum(lq1_ref[...] * lk1_ref[...], axis=1, keepdims=True))
           - jnp.exp(jnp.sum(lq2_ref[...] * lk2_ref[...], axis=1, keepdims=True))
           + lambda_init)
    acc1 = acc_ref[0]
    acc2 = acc_ref[1]
    out_t = (acc1[:LANES] * (1.0 / acc1[LANES:LANES + 1])
             - lam * (acc2[:LANES] * (1.0 / acc2[LANES:LANES + 1])))
    out = _rms(out_t.T, g_ref[...]) * (1.0 - lambda_init)
    o_ref[...] = out.astype(o_ref.dtype)


def _diff_attention(proj, vt, lq1, lk1, lq2, lk2, subln_g, batch, seq, tq, lambda_init):
    n = proj.shape[0]
    nq = seq // tq
    q_off = 2 * SB_WIDTH // LANES
    k_off = q_off + DIFF_WIDTH // LANES
    vec = _const_spec((1, HEAD_DIM))
    return pl.pallas_call(
        functools.partial(_diff_kernel, tq=tq, lambda_init=lambda_init),
        grid=(batch, DIFF_HEADS, nq),
        in_specs=[
            vec, vec, vec, vec,
            _const_spec((1, LANES)),
            pl.BlockSpec((tq, LANES), lambda b, h, i: (b * nq + i, q_off + h)),
            pl.BlockSpec((seq, LANES), lambda b, h, i: (b, k_off + h)),
            pl.BlockSpec((None, None, LANES, seq), lambda b, h, i: (b, h, 0, 0)),
        ],
        out_specs=pl.BlockSpec((tq, LANES), lambda b, h, i: (b * nq + i, h)),
        out_shape=jax.ShapeDtypeStruct((n, DIFF_WIDTH), BF16),
        scratch_shapes=[pltpu.VMEM((2, 1, tq), F32),
                        pltpu.VMEM((2, LANES + ONES_ROWS, tq), F32),
                        pltpu.VMEM((2, 2, tq, tq), F32),
                        pltpu.VMEM((2, 2, 1, tq), F32)],
        compiler_params=_params("arbitrary", "arbitrary", "arbitrary"),
        name="diff_attention",
    )(lq1, lk1, lq2, lk2, subln_g, proj, proj, vt)


def _out_ffn_kernel(x_ref, a_ref, b_ref, wo_ref, gf_ref, wgu_ref, wd_ref, gn_ref, o_ref,
                    acc_ref, *, final_norm):
    half = a_ref.shape[1]
    x1 = x_ref[...] + _dot(a_ref[...], wo_ref[:half, :]) + _dot(b_ref[...], wo_ref[half:, :])
    h = _rms(x1, gf_ref[...]).astype(BF16)
    acc_ref[...] = x1
    fc = wd_ref.shape[1]

    def body(c, carry):
        gu = _dot(h, wgu_ref[c])
        gate = gu[:, :fc]
        act = (gate * _sigmoid(gate) * gu[:, fc:]).astype(BF16)
        acc_ref[...] += _dot(act, wd_ref[c])
        return carry

    lax.fori_loop(0, wd_ref.shape[0], body, 0)
    y = acc_ref[...]
    if final_norm:
        y = _rms(y, gn_ref[...])
    o_ref[...] = y


def _out_ffn(x2, a, b, wo, gf, wgu, wd, gn, tm, final_norm):
    n = x2.shape[0]
    half = a.shape[1]
    nc, _, fc2 = wgu.shape
    return pl.pallas_call(
        functools.partial(_out_ffn_kernel, final_norm=final_norm),
        grid=(n // tm,),
        in_specs=[
            pl.BlockSpec((tm, D_MODEL), lambda i: (i, 0)),
            pl.BlockSpec((tm, half), lambda i: (i, 0)),
            pl.BlockSpec((tm, half), lambda i: (i, 0)),
            _const_spec((2 * half, D_MODEL)),
            _const_spec((1, D_MODEL)),
            _const_spec((nc, D_MODEL, fc2)),
            _const_spec((nc, fc2 // 2, D_MODEL)),
            _const_spec((1, D_MODEL)),
        ],
        out_specs=pl.BlockSpec((tm, D_MODEL), lambda i: (i, 0)),
        out_shape=jax.ShapeDtypeStruct((n, D_MODEL), F32),
        scratch_shapes=[pltpu.VMEM((tm, D_MODEL), F32)],
        compiler_params=_params("arbitrary"),
        name="out_proj_ffn",
    )(x2, a, b, wo, gf, wgu, wd, gn)


def _l1_mixer_kernel(x_ref, g_ref, w_ref, pw_ref, ps_ref, dw_ref, db_ref, cg_ref, cb_ref,
                     c_ref, d_ref, ext_p, ext_u, shift_ref, conv_ref):
    ti = pl.program_id(1)
    tm = x_ref.shape[0]

    @pl.when(ti == 0)
    def _():
        ext_p[:HALO, :] = jnp.zeros((HALO, POOL_WIDTH), F32)
        ext_u[:HALO, :] = jnp.zeros((HALO, CONV_WIDTH), F32)

    h = _rms(x_ref[...], g_ref[...]).astype(BF16)
    xp = _dot(h, w_ref[:, :POOL_WIDTH])
    xa = _dot(h, w_ref[:, POOL_WIDTH:POOL_WIDTH + CONV_WIDTH])
    xg = _dot(h, w_ref[:, POOL_WIDTH + CONV_WIDTH:])
    ext_p[HALO:, :] = xp
    ext_u[HALO:, :] = xa * _sigmoid(xg)

    pos1 = (ti * tm + 1 + lax.broadcasted_iota(jnp.int32, (tm, 1), 0)).astype(F32)
    for g, win in enumerate(POOL_WINDOWS):
        cols = slice(g * POOL_GROUP_DIM, (g + 1) * POOL_GROUP_DIM)
        total = ext_p[HALO:, cols]
        for back in range(1, win):
            total = total + ext_p[HALO - back:HALO - back + tm, cols]
        pooled = total / jnp.minimum(pos1, float(win)) - ext_p[HALO:, cols]
        mixed = _dot(pooled.astype(BF16), pw_ref[g])
        c_ref[:, cols] = (mixed * ps_ref[:, cols]).astype(c_ref.dtype)

    for b in range(1, SUBLANES):
        shift_ref[b - 1, SUBLANES:, :] = ext_u[SUBLANES - b:HALO + tm - b, :]

    def row_block(rb, carry):
        r0 = pl.multiple_of(rb * CONV_ROWS, CONV_ROWS)
        conv = jnp.zeros((CONV_ROWS, CONV_WIDTH), F32) + db_ref[...]
        for back in range(CONV_KERNEL):
            a, b = divmod(back, SUBLANES)
            src = ext_u if b == 0 else shift_ref.at[b - 1]
            tap = CONV_KERNEL - 1 - back
            conv = conv + (src[pl.ds(r0 + HALO - SUBLANES * a, CONV_ROWS), :]
                           * jnp.concatenate([dw_ref[tap]] * (CONV_ROWS // SUBLANES), axis=0))
        conv_ref[pl.ds(r0, CONV_ROWS), :] = conv
        return carry

    lax.fori_loop(0, tm // CONV_ROWS, row_block, 0)
    conv = conv_ref[...]
    mu = jnp.mean(conv, axis=-1, keepdims=True)
    cen = conv - mu
    var = jnp.mean(cen * cen, axis=-1, keepdims=True)
    y = cen * lax.rsqrt(var + EPS) * cg_ref[...] + cb_ref[...]
    d_ref[...] = (y * _sigmoid(y)).astype(d_ref.dtype)

    ext_p[:HALO, :] = ext_p[tm:, :]
    ext_u[:HALO, :] = ext_u[tm:, :]


def _l1_mixer(x2, g, w, pw, ps, dw, db, cg, cb, batch, seq, tm):
    n = x2.shape[0]
    nt = seq // tm
    row = lambda b, t: (b * nt + t, 0)
    return pl.pallas_call(
        _l1_mixer_kernel,
        grid=(batch, nt),
        in_specs=[
            pl.BlockSpec((tm, D_MODEL), row),
            _const_spec((1, D_MODEL)),
            _const_spec((D_MODEL, ODD_IN_WIDTH)),
            _const_spec((len(POOL_WINDOWS), POOL_GROUP_DIM, POOL_GROUP_DIM)),
            _const_spec((1, POOL_WIDTH)),
            _const_spec((CONV_KERNEL, SUBLANES, CONV_WIDTH)),
            _const_spec((1, CONV_WIDTH)),
            _const_spec((1, CONV_WIDTH)),
            _const_spec((1, CONV_WIDTH)),
        ],
        out_specs=[pl.BlockSpec((tm, POOL_WIDTH), row), pl.BlockSpec((tm, CONV_WIDTH), row)],
        out_shape=[jax.ShapeDtypeStruct((n, POOL_WIDTH), BF16),
                   jax.ShapeDtypeStruct((n, CONV_WIDTH), BF16)],
        scratch_shapes=[pltpu.VMEM((HALO + tm, POOL_WIDTH), F32),
                        pltpu.VMEM((HALO + tm, CONV_WIDTH), F32),
                        pltpu.VMEM((SUBLANES - 1, HALO + tm, CONV_WIDTH), F32),
                        pltpu.VMEM((tm, CONV_WIDTH), F32)],
        compiler_params=_params("arbitrary", "arbitrary"),
        name="l1_mixer",
    )(x2, g, w, pw, ps, dw, db, cg, cb)


def _ffn_weights(w_gate, w_up, w_down):
    nc = D_FF // FF_CHUNK
    wg = w_gate.astype(BF16).reshape(D_MODEL, nc, FF_CHUNK)
    wu = w_up.astype(BF16).reshape(D_MODEL, nc, FF_CHUNK)
    wgu = jnp.concatenate([wg, wu], axis=2).transpose(1, 0, 2)
    wd = w_down.astype(BF16).reshape(nc, FF_CHUNK, D_MODEL)
    return wgu, wd


def _rope_tables(seq):
    half = HEAD_DIM // 2
    inv = jnp.power(ROPE_THETA, -jnp.arange(0, HEAD_DIM, 2, dtype=F32) / HEAD_DIM)
    ang = jnp.arange(seq, dtype=jnp.int32).astype(F32)[:, None] * inv[None, :]
    cos = jnp.cos(ang)
    sin = jnp.sin(ang)
    reps = LANES // HEAD_DIM
    return (jnp.tile(jnp.concatenate([cos, cos], axis=1), (1, reps)),
            jnp.tile(jnp.concatenate([-sin, sin], axis=1), (1, reps)))


def _tile(seq, want):
    return want if seq % want == 0 else seq


def kernel(x, mix_norm_0, w_in_0, lambda_q1_0, lambda_k1_0, lambda_q2_0, lambda_k2_0, subln_0, w_out_0, ffn_norm_0, w_gate_0, w_up_0, w_down_0, mix_norm_1, w_in_1, pool_w_1, pool_scale_1, dw_w_1, dw_b_1, conv_norm_g_1, conv_norm_b_1, w_out_1, ffn_norm_1, w_gate_1, w_up_1, w_down_1, final_norm):
    batch, seq, d = x.shape
    assert d == D_MODEL and seq % LANES == 0
    x2 = x.reshape(batch * seq, d)
    row = lambda v: v.reshape(1, -1).astype(F32)
    tm = _tile(seq, 512)

    cos_t, sin_t = _rope_tables(seq)
    proj, svt, dvt = _in_proj0(x2, row(mix_norm_0), w_in_0.astype(BF16), cos_t, sin_t,
                               batch, seq, tm)
    a_out = _sb_attention(proj, svt, batch, seq, _tile(seq, 256))
    lambda_init = 0.8 - 0.6 * math.exp(-0.3 * 0)
    b_out = _diff_attention(proj, dvt, row(lambda_q1_0), row(lambda_k1_0), row(lambda_q2_0),
                            row(lambda_k2_0), row(subln_0), batch, seq, _tile(seq, 512),
                            lambda_init)
    wgu0, wd0 = _ffn_weights(w_gate_0, w_up_0, w_down_0)
    x2 = _out_ffn(x2, a_out, b_out, w_out_0.astype(BF16), row(ffn_norm_0), wgu0, wd0,
                  row(final_norm), tm, final_norm=False)

    c_out, d_out = _l1_mixer(x2, row(mix_norm_1), w_in_1.astype(BF16), pool_w_1.astype(BF16),
                             row(pool_scale_1),
                             jnp.broadcast_to(dw_w_1.astype(F32)[:, None, :],
                                              (CONV_KERNEL, SUBLANES, CONV_WIDTH)),
                             row(dw_b_1),
                             row(conv_norm_g_1), row(conv_norm_b_1), batch, seq, tm)
    wgu1, wd1 = _ffn_weights(w_gate_1, w_up_1, w_down_1)
    out = _out_ffn(x2, c_out, d_out, w_out_1.astype(BF16), row(ffn_norm_1), wgu1, wd1,
                   row(final_norm), tm, final_norm=True)
    return out.reshape(batch, seq, d)
```

```python
import functools
import math

import jax
import jax.numpy as jnp
from jax import lax
from jax.experimental import pallas as pl
from jax.experimental.pallas import tpu as pltpu

F32 = jnp.float32
BF16 = jnp.bfloat16

D_MODEL = 1024
HEAD_DIM = 64
LANES = 128
SUBLANES = 8
CONV_ROWS = 64
SB_HEADS = 8
DIFF_HEADS = 4
SB_WIDTH = SB_HEADS * HEAD_DIM
DIFF_WIDTH = DIFF_HEADS * 2 * HEAD_DIM
EVEN_IN_WIDTH = 3 * SB_WIDTH + 3 * DIFF_WIDTH
CHUNK = 64
POOL_WINDOWS = (2, 4, 8, 16)
POOL_WIDTH = 512
POOL_GROUP_DIM = 128
CONV_WIDTH = 512
CONV_KERNEL = 31
ODD_IN_WIDTH = POOL_WIDTH + 2 * CONV_WIDTH
D_FF = 2816
FF_CHUNK = 256
ROPE_THETA = 10000.0
EPS = 1e-6
NEG = -1e30
SB_SCALE = HEAD_DIM ** -0.5
LOG2E = math.log2(math.e)
ONES_ROWS = 16
HALO = 32
QK_WIDTH = 2 * SB_WIDTH + 2 * DIFF_WIDTH
SB_LOG2_ZERO = -150.0 * math.log2(math.e)
VMEM_LIMIT = 56 * 1024 * 1024


def _rms(x, g):
    return x * lax.rsqrt(jnp.mean(x * x, axis=-1, keepdims=True) + EPS) * g


def _sigmoid(x):
    return 1.0 / (1.0 + jnp.exp(-x))


def _dot(a, b):
    return jnp.dot(a, b, preferred_element_type=F32)


def _dot_nt(a, b):
    return lax.dot_general(a, b, (((1,), (1,)), ((), ())), preferred_element_type=F32)


def _params(*sem):
    return pltpu.CompilerParams(dimension_semantics=sem, vmem_limit_bytes=VMEM_LIMIT)


def _const_spec(shape):
    nd = len(shape)
    return pl.BlockSpec(shape, lambda *_: (0,) * nd)


def _in_proj0_kernel(x_ref, g_ref, w_ref, cos_ref, sin_ref, o_ref, svt_ref, dvt_ref):
    h = _rms(x_ref[...], g_ref[...]).astype(BF16)
    tm = h.shape[0]
    lane = lax.broadcasted_iota(jnp.int32, (tm, LANES), 1)
    first_half = (lane % HEAD_DIM) < (HEAD_DIM // 2)
    cos = cos_ref[...]
    sin = sin_ref[...]
    seg = SB_WIDTH
    q_scale = SB_SCALE * LOG2E
    out_col = {0: 0, 1: 1, 3: 2, 4: 3}
    for c in range(EVEN_IN_WIDTH // seg):
        y = _dot(h, w_ref[:, c * seg:(c + 1) * seg])
        for s in range(seg // LANES):
            t = y[:, s * LANES:(s + 1) * LANES]
            if c in (2, 5):
                (svt_ref if c == 2 else dvt_ref)[0, s] = t.T.astype(BF16)
                continue
            if c in (3, 4):
                swapped = jnp.where(first_half,
                                    pltpu.roll(t, LANES - HEAD_DIM // 2, 1),
                                    pltpu.roll(t, HEAD_DIM // 2, 1))
                t = t * cos + swapped * sin
            if c in (0, 3):
                t = t * q_scale
            col = out_col[c] * seg + s * LANES
            o_ref[:, col:col + LANES] = t.astype(BF16)


def _in_proj0(x2, g, w, cos_t, sin_t, batch, seq, tm):
    n = x2.shape[0]
    per_seq = seq // tm
    vt_spec = pl.BlockSpec((1, SB_WIDTH // LANES, LANES, tm),
                           lambda i: (i // per_seq, 0, 0, i % per_seq))
    vt_shape = jax.ShapeDtypeStruct((batch, SB_WIDTH // LANES, LANES, seq), BF16)
    return pl.pallas_call(
        _in_proj0_kernel,
        grid=(n // tm,),
        in_specs=[
            pl.BlockSpec((tm, D_MODEL), lambda i: (i, 0)),
            _const_spec((1, D_MODEL)),
            _const_spec((D_MODEL, EVEN_IN_WIDTH)),
            pl.BlockSpec((tm, LANES), lambda i: (i % per_seq, 0)),
            pl.BlockSpec((tm, LANES), lambda i: (i % per_seq, 0)),
        ],
        out_specs=[pl.BlockSpec((tm, QK_WIDTH), lambda i: (i, 0)), vt_spec, vt_spec],
        out_shape=[jax.ShapeDtypeStruct((n, QK_WIDTH), BF16), vt_shape, vt_shape],
        compiler_params=_params("arbitrary"),
        name="l0_in_proj",
    )(x2, g, w, cos_t, sin_t)


def _sb_kernel(q_ref, k_ref, vt_ref, o_ref, acc_ref, rest_ref, *, tq):
    lane = lax.broadcasted_iota(jnp.int32, (tq, LANES), 1)
    low = lane < HEAD_DIM
    tri_row = lax.broadcasted_iota(jnp.int32, (tq + ONES_ROWS, tq), 0)
    tri_col = lax.broadcasted_iota(jnp.int32, (tq + ONES_ROWS, tq), 1)
    tri = jnp.logical_or(tri_col > tri_row, tri_row >= tq).astype(BF16)
    head0_rows = lax.broadcasted_iota(jnp.int32, (LANES, tq), 0) < HEAD_DIM

    def sweep(q2, first, nblk, diagonal):
        nk = nblk * tq
        start = pl.multiple_of(first * tq, tq)
        z = _dot_nt(k_ref[pl.ds(start, nk), :], q2)
        log_1m = jnp.minimum(-z, 0.0) - jnp.log2(1.0 + jnp.exp2(-jnp.abs(z)))
        if diagonal:
            key = lax.broadcasted_iota(jnp.int32, (nk, 2 * tq), 0) - (nk - tq)
            qry = jnp.bitwise_and(lax.broadcasted_iota(jnp.int32, (nk, 2 * tq), 1), tq - 1)
            strict = key < qry
            log_1m = jnp.where(strict, log_1m, 0.0)
        hi = log_1m.astype(BF16)
        lo = (log_1m - hi.astype(F32)).astype(BF16)
        rest = rest_ref[...]
        log_w = []
        for b in reversed(range(nblk)):
            rows = slice(b * tq, (b + 1) * tq)
            after = _dot(tri, hi[rows]) + _dot(tri, lo[rows])
            log_w.append(z[rows] + log_1m[rows] + after[:tq] + rest)
            rest = rest + after[tq:tq + 1]
        w = jnp.exp2(jnp.concatenate(log_w[::-1], axis=0))
        if diagonal:
            w = jnp.where(strict, w, 0.0)
        acc_ref[...] += _dot(vt_ref[:, pl.ds(start, nk)], w.astype(BF16))
        rest_ref[...] = rest

    def query_block(i, first_nblk):
        rows = pl.ds(pl.multiple_of(i * tq, tq), tq)
        q = q_ref[rows, :]
        zero = jnp.zeros_like(q)
        q2 = jnp.concatenate([jnp.where(low, q, zero), jnp.where(low, zero, q)], axis=0)
        acc_ref[...] = jnp.zeros_like(acc_ref)
        rest_ref[...] = jnp.zeros_like(rest_ref)
        sweep(q2, i + 1 - first_nblk, first_nblk, True)

        def cond(st):
            j, worst = st
            return jnp.logical_and(j >= 0, worst > SB_LOG2_ZERO)

        def body(st):
            j, _ = st
            sweep(q2, j, 1, False)
            return j - 1, jnp.max(rest_ref[...])

        lax.while_loop(cond, body, (i - first_nblk, jnp.max(rest_ref[...])))
        acc = acc_ref[...]
        o_ref[rows, :] = jnp.where(head0_rows, acc[:, :tq], acc[:, tq:]).T.astype(o_ref.dtype)

    query_block(0, 1)

    def step(i, carry):
        query_block(i, 2)
        return carry

    lax.fori_loop(1, q_ref.shape[0] // tq, step, 0)


def _sb_attention(proj, vt, batch, seq, tq):
    n = proj.shape[0]
    k_off = SB_WIDTH // LANES
    return pl.pallas_call(
        functools.partial(_sb_kernel, tq=tq),
        grid=(batch, SB_WIDTH // LANES),
        in_specs=[
            pl.BlockSpec((seq, LANES), lambda b, p: (b, p)),
            pl.BlockSpec((seq, LANES), lambda b, p: (b, k_off + p)),
            pl.BlockSpec((None, None, LANES, seq), lambda b, p: (b, p, 0, 0)),
        ],
        out_specs=pl.BlockSpec((seq, LANES), lambda b, p: (b, p)),
        out_shape=jax.ShapeDtypeStruct((n, SB_WIDTH), BF16),
        scratch_shapes=[pltpu.VMEM((LANES, 2 * tq), F32), pltpu.VMEM((1, 2 * tq), F32)],
        compiler_params=_params("arbitrary", "arbitrary"),
        name="sb_attention",
    )(proj, proj, vt)


def _diff_kernel(lq1_ref, lk1_ref, lq2_ref, lk2_ref, g_ref, q_ref, k_ref, vt_ref, o_ref,
                 m_ref, acc_ref, s_ref, bmax_ref, *, tq, lambda_init):
    lane = lax.broadcasted_iota(jnp.int32, (tq, LANES), 1)
    low = lane < HEAD_DIM
    ones = jnp.ones((ONES_ROWS, tq), BF16)
    lam = (jnp.exp(jnp.sum(lq1_ref[...] * lk1_ref[...], axis=1, keepdims=True))
           - jnp.exp(jnp.sum(lq2_ref[...] * lk2_ref[...], axis=1, keepdims=True))
           + lambda_init)

    def query_block(i, carry):
        rows = pl.ds(pl.multiple_of(i * tq, tq), tq)
        q = q_ref[rows, :]
        zero = jnp.zeros_like(q)
        q_maps = (jnp.where(low, q, zero), jnp.where(low, zero, q))
        m_ref[...] = jnp.full_like(m_ref, NEG)
        acc_ref[...] = jnp.zeros_like(acc_ref)

        def scores(j, buf, diagonal=False):
            k = k_ref[pl.ds(pl.multiple_of(j * tq, tq), tq), :]
            for mp in range(2):
                s = _dot_nt(k, q_maps[mp])
                if diagonal:
                    visible = ((lax.broadcasted_iota(jnp.int32, (tq, tq), 0) // CHUNK)
                               <= (lax.broadcasted_iota(jnp.int32, (tq, tq), 1) // CHUNK))
                    s = jnp.where(visible, s, NEG)
                s_ref[buf, mp] = s
                bmax_ref[buf, mp] = jnp.max(s, axis=0, keepdims=True)

        def accumulate(j, buf):
            vt = jnp.concatenate([vt_ref[:, pl.ds(pl.multiple_of(j * tq, tq), tq)], ones],
                                 axis=0)
            for mp in range(2):
                m_old = m_ref[mp]
                m_new = jnp.maximum(m_old, bmax_ref[buf, mp])
                p = jnp.exp2(s_ref[buf, mp] - m_new).astype(BF16)
                acc_ref[mp] = jnp.exp2(m_old - m_new) * acc_ref[mp] + _dot(vt, p)
                m_ref[mp] = m_new

        scores(i, 0, diagonal=True)
        pairs = i // 2

        def body(jj, c):
            scores(2 * jj, 1)
            accumulate(jnp.where(jj == 0, i, 2 * jj - 1), 0)
            scores(2 * jj + 1, 0)
            accumulate(2 * jj, 1)
            return c

        lax.fori_loop(0, pairs, body, 0)
        held = jnp.where(pairs == 0, i, 2 * pairs - 1)

        @pl.when(i % 2 == 1)
        def _():
            scores(i - 1, 1)
            accumulate(held, 0)
            accumulate(i - 1, 1)

        @pl.when(i % 2 == 0)
        def _():
            accumulate(held, 0)

        acc1 = acc_ref[0]
        acc2 = acc_ref[1]
        out_t = (acc1[:LANES] * (1.0 / acc1[LANES:LANES + 1])
                 - lam * (acc2[:LANES] * (1.0 / acc2[LANES:LANES + 1])))
        out = _rms(out_t.T, g_ref[...]) * (1.0 - lambda_init)
        o_ref[rows, :] = out.astype(o_ref.dtype)
        return carry

    lax.fori_loop(0, q_ref.shape[0] // tq, query_block, 0)


def _diff_attention(proj, vt, lq1, lk1, lq2, lk2, subln_g, batch, seq, tq, lambda_init):
    n = proj.shape[0]
    q_off = 2 * SB_WIDTH // LANES
    k_off = q_off + DIFF_WIDTH // LANES
    vec = _const_spec((1, HEAD_DIM))
    return pl.pallas_call(
        functools.partial(_diff_kernel, tq=tq, lambda_init=lambda_init),
        grid=(batch, DIFF_HEADS),
        in_specs=[
            vec, vec, vec, vec,
            _const_spec((1, LANES)),
            pl.BlockSpec((seq, LANES), lambda b, h: (b, q_off + h)),
            pl.BlockSpec((seq, LANES), lambda b, h: (b, k_off + h)),
            pl.BlockSpec((None, None, LANES, seq), lambda b, h: (b, h, 0, 0)),
        ],
        out_specs=pl.BlockSpec((seq, LANES), lambda b, h: (b, h)),
        out_shape=jax.ShapeDtypeStruct((n, DIFF_WIDTH), BF16),
        scratch_shapes=[pltpu.VMEM((2, 1, tq), F32),
                        pltpu.VMEM((2, LANES + ONES_ROWS, tq), F32),
                        pltpu.VMEM((2, 2, tq, tq), F32),
                        pltpu.VMEM((2, 2, 1, tq), F32)],
        compiler_params=_params("arbitrary", "arbitrary"),
        name="diff_attention",
    )(lq1, lk1, lq2, lk2, subln_g, proj, proj, vt)


def _out_ffn_kernel(x_ref, a_ref, b_ref, wo_ref, gf_ref, wgu_ref, wd_ref, gn_ref, o_ref,
                    acc_ref, *, final_norm):
    half = a_ref.shape[1]
    x1 = x_ref[...] + _dot(a_ref[...], wo_ref[:half, :]) + _dot(b_ref[...], wo_ref[half:, :])
    h = _rms(x1, gf_ref[...]).astype(BF16)
    acc_ref[...] = x1
    fc = wd_ref.shape[1]

    def body(c, carry):
        gu = _dot(h, wgu_ref[c])
        gate = gu[:, :fc]
        act = (gate * _sigmoid(gate) * gu[:, fc:]).astype(BF16)
        acc_ref[...] += _dot(act, wd_ref[c])
        return carry

    lax.fori_loop(0, wd_ref.shape[0], body, 0)
    y = acc_ref[...]
    if final_norm:
        y = _rms(y, gn_ref[...])
    o_ref[...] = y


def _out_ffn(x2, a, b, wo, gf, wgu, wd, gn, tm, final_norm):
    n = x2.shape[0]
    half = a.shape[1]
    nc, _, fc2 = wgu.shape
    return pl.pallas_call(
        functools.partial(_out_ffn_kernel, final_norm=final_norm),
        grid=(n // tm,),
        in_specs=[
            pl.BlockSpec((tm, D_MODEL), lambda i: (i, 0)),
            pl.BlockSpec((tm, half), lambda i: (i, 0)),
            pl.BlockSpec((tm, half), lambda i: (i, 0)),
            _const_spec((2 * half, D_MODEL)),
            _const_spec((1, D_MODEL)),
            _const_spec((nc, D_MODEL, fc2)),
            _const_spec((nc, fc2 // 2, D_MODEL)),
            _const_spec((1, D_MODEL)),
        ],
        out_specs=pl.BlockSpec((tm, D_MODEL), lambda i: (i, 0)),
        out_shape=jax.ShapeDtypeStruct((n, D_MODEL), F32),
        scratch_shapes=[pltpu.VMEM((tm, D_MODEL), F32)],
        compiler_params=_params("arbitrary"),
        name="out_proj_ffn",
    )(x2, a, b, wo, gf, wgu, wd, gn)


def _l1_mixer_kernel(x_ref, g_ref, w_ref, pw_ref, ps_ref, dw_ref, db_ref, cg_ref, cb_ref,
                     c_ref, d_ref, ext_p, ext_u, shift_ref, conv_ref):
    ti = pl.program_id(1)
    tm = x_ref.shape[0]

    @pl.when(ti == 0)
    def _():
        ext_p[:HALO, :] = jnp.zeros((HALO, POOL_WIDTH), F32)
        ext_u[:HALO, :] = jnp.zeros((HALO, CONV_WIDTH), F32)

    h = _rms(x_ref[...], g_ref[...]).astype(BF16)
    xp = _dot(h, w_ref[:, :POOL_WIDTH])
    xa = _dot(h, w_ref[:, POOL_WIDTH:POOL_WIDTH + CONV_WIDTH])
    xg = _dot(h, w_ref[:, POOL_WIDTH + CONV_WIDTH:])
    ext_p[HALO:, :] = xp
    ext_u[HALO:, :] = xa * _sigmoid(xg)

    pos1 = (ti * tm + 1 + lax.broadcasted_iota(jnp.int32, (tm, 1), 0)).astype(F32)
    for g, win in enumerate(POOL_WINDOWS):
        cols = slice(g * POOL_GROUP_DIM, (g + 1) * POOL_GROUP_DIM)
        total = ext_p[HALO:, cols]
        for back in range(1, win):
            total = total + ext_p[HALO - back:HALO - back + tm, cols]
        pooled = total / jnp.minimum(pos1, float(win)) - ext_p[HALO:, cols]
        mixed = _dot(pooled.astype(BF16), pw_ref[g])
        c_ref[:, cols] = (mixed * ps_ref[:, cols]).astype(c_ref.dtype)

    for b in range(1, SUBLANES):
        shift_ref[b - 1, SUBLANES:, :] = ext_u[SUBLANES - b:HALO + tm - b, :]

    def row_block(rb, carry):
        r0 = pl.multiple_of(rb * CONV_ROWS, CONV_ROWS)
        conv = jnp.zeros((CONV_ROWS, CONV_WIDTH), F32) + db_ref[...]
        for back in range(CONV_KERNEL):
            a, b = divmod(back, SUBLANES)
            src = ext_u if b == 0 else shift_ref.at[b - 1]
            tap = CONV_KERNEL - 1 - back
            conv = conv + (src[pl.ds(r0 + HALO - SUBLANES * a, CONV_ROWS), :]
                           * jnp.concatenate([dw_ref[tap]] * (CONV_ROWS // SUBLANES), axis=0))
        conv_ref[pl.ds(r0, CONV_ROWS), :] = conv
        return carry

    lax.fori_loop(0, tm // CONV_ROWS, row_block, 0)
    conv = conv_ref[...]
    mu = jnp.mean(conv, axis=-1, keepdims=True)
    cen = conv - mu
    var = jnp.mean(cen * cen, axis=-1, keepdims=True)
    y = cen * lax.rsqrt(var + EPS) * cg_ref[...] + cb_ref[...]
    d_ref[...] = (y * _sigmoid(y)).astype(d_ref.dtype)

    ext_p[:HALO, :] = ext_p[tm:, :]
    ext_u[:HALO, :] = ext_u[tm:, :]


def _l1_mixer(x2, g, w, pw, ps, dw, db, cg, cb, batch, seq, tm):
    n = x2.shape[0]
    nt = seq // tm
    row = lambda b, t: (b * nt + t, 0)
    return pl.pallas_call(
        _l1_mixer_kernel,
        grid=(batch, nt),
        in_specs=[
            pl.BlockSpec((tm, D_MODEL), row),
            _const_spec((1, D_MODEL)),
            _const_spec((D_MODEL, ODD_IN_WIDTH)),
            _const_spec((len(POOL_WINDOWS), POOL_GROUP_DIM, POOL_GROUP_DIM)),
            _const_spec((1, POOL_WIDTH)),
            _const_spec((CONV_KERNEL, SUBLANES, CONV_WIDTH)),
            _const_spec((1, CONV_WIDTH)),
            _const_spec((1, CONV_WIDTH)),
            _const_spec((1, CONV_WIDTH)),
        ],
        out_specs=[pl.BlockSpec((tm, POOL_WIDTH), row), pl.BlockSpec((tm, CONV_WIDTH), row)],
        out_shape=[jax.ShapeDtypeStruct((n, POOL_WIDTH), BF16),
                   jax.ShapeDtypeStruct((n, CONV_WIDTH), BF16)],
        scratch_shapes=[pltpu.VMEM((HALO + tm, POOL_WIDTH), F32),
                        pltpu.VMEM((HALO + tm, CONV_WIDTH), F32),
                        pltpu.VMEM((SUBLANES - 1, HALO + tm, CONV_WIDTH), F32),
                        pltpu.VMEM((tm, CONV_WIDTH), F32)],
        compiler_params=_params("arbitrary", "arbitrary"),
        name="l1_mixer",
    )(x2, g, w, pw, ps, dw, db, cg, cb)


def _ffn_weights(w_gate, w_up, w_down):
    nc = D_FF // FF_CHUNK
    wg = w_gate.astype(BF16).reshape(D_MODEL, nc, FF_CHUNK)
    wu = w_up.astype(BF16).reshape(D_MODEL, nc, FF_CHUNK)
    wgu = jnp.concatenate([wg, wu], axis=2).transpose(1, 0, 2)
    wd = w_down.astype(BF16).reshape(nc, FF_CHUNK, D_MODEL)
    return wgu, wd


def _rope_tables(seq):
    half = HEAD_DIM // 2
    inv = jnp.power(ROPE_THETA, -jnp.arange(0, HEAD_DIM, 2, dtype=F32) / HEAD_DIM)
    ang = jnp.arange(seq, dtype=jnp.int32).astype(F32)[:, None] * inv[None, :]
    cos = jnp.cos(ang)
    sin = jnp.sin(ang)
    reps = LANES // HEAD_DIM
    return (jnp.tile(jnp.concatenate([cos, cos], axis=1), (1, reps)),
            jnp.tile(jnp.concatenate([-sin, sin], axis=1), (1, reps)))


def _tile(seq, want):
    return want if seq % want == 0 else seq


def kernel(x, mix_norm_0, w_in_0, lambda_q1_0, lambda_k1_0, lambda_q2_0, lambda_k2_0, subln_0, w_out_0, ffn_norm_0, w_gate_0, w_up_0, w_down_0, mix_norm_1, w_in_1, pool_w_1, pool_scale_1, dw_w_1, dw_b_1, conv_norm_g_1, conv_norm_b_1, w_out_1, ffn_norm_1, w_gate_1, w_up_1, w_down_1, final_norm):
    batch, seq, d = x.shape
    assert d == D_MODEL and seq % LANES == 0
    x2 = x.reshape(batch * seq, d)
    row = lambda v: v.reshape(1, -1).astype(F32)
    tm = _tile(seq, 512)

    cos_t, sin_t = _rope_tables(seq)
    proj, svt, dvt = _in_proj0(x2, row(mix_norm_0), w_in_0.astype(BF16), cos_t, sin_t,
                               batch, seq, tm)
    a_out = _sb_attention(proj, svt, batch, seq, _tile(seq, 256))
    lambda_init = 0.8 - 0.6 * math.exp(-0.3 * 0)
    b_out = _diff_attention(proj, dvt, row(lambda_q1_0), row(lambda_k1_0), row(lambda_q2_0),
                            row(lambda_k2_0), row(subln_0), batch, seq, _tile(seq, 512),
                            lambda_init)
    wgu0, wd0 = _ffn_weights(w_gate_0, w_up_0, w_down_0)
    x2 = _out_ffn(x2, a_out, b_out, w_out_0.astype(BF16), row(ffn_norm_0), wgu0, wd0,
                  row(final_norm), tm, final_norm=False)

    c_out, d_out = _l1_mixer(x2, row(mix_norm_1), w_in_1.astype(BF16), pool_w_1.astype(BF16),
                             row(pool_scale_1),
                             jnp.broadcast_to(dw_w_1.astype(F32)[:, None, :],
                                              (CONV_KERNEL, SUBLANES, CONV_WIDTH)),
                             row(dw_b_1),
                             row(conv_norm_g_1), row(conv_norm_b_1), batch, seq, tm)
    wgu1, wd1 = _ffn_weights(w_gate_1, w_up_1, w_down_1)
    out = _out_ffn(x2, c_out, d_out, w_out_1.astype(BF16), row(ffn_norm_1), wgu1, wd1,
                   row(final_norm), tm, final_norm=True)
    return out.reshape(batch, seq, d)
```

```python
import functools
import math

import jax
import jax.numpy as jnp
from jax import lax
from jax.experimental import pallas as pl
from jax.experimental.pallas import tpu as pltpu

F32 = jnp.float32
BF16 = jnp.bfloat16

D_MODEL = 1024
HEAD_DIM = 64
LANES = 128
SUBLANES = 8
CONV_ROWS = 64
SB_HEADS = 8
DIFF_HEADS = 4
SB_WIDTH = SB_HEADS * HEAD_DIM
DIFF_WIDTH = DIFF_HEADS * 2 * HEAD_DIM
EVEN_IN_WIDTH = 3 * SB_WIDTH + 3 * DIFF_WIDTH
CHUNK = 64
POOL_WINDOWS = (2, 4, 8, 16)
POOL_WIDTH = 512
POOL_GROUP_DIM = 128
CONV_WIDTH = 512
CONV_KERNEL = 31
ODD_IN_WIDTH = POOL_WIDTH + 2 * CONV_WIDTH
D_FF = 2816
FF_CHUNK = 256
ROPE_THETA = 10000.0
EPS = 1e-6
NEG = -1e30
SB_SCALE = HEAD_DIM ** -0.5
LOG2E = math.log2(math.e)
ONES_ROWS = 16
HALO = 32
QK_WIDTH = 2 * SB_WIDTH + 2 * DIFF_WIDTH
SB_LOG2_ZERO = -150.0 * math.log2(math.e)
VMEM_LIMIT = 56 * 1024 * 1024


def _rms(x, g):
    return x * lax.rsqrt(jnp.mean(x * x, axis=-1, keepdims=True) + EPS) * g


def _sigmoid(x):
    return 1.0 / (1.0 + jnp.exp(-x))


def _dot(a, b):
    return jnp.dot(a, b, preferred_element_type=F32)


def _dot_nt(a, b):
    return lax.dot_general(a, b, (((1,), (1,)), ((), ())), preferred_element_type=F32)


def _params(*sem):
    return pltpu.CompilerParams(dimension_semantics=sem, vmem_limit_bytes=VMEM_LIMIT)


def _const_spec(shape, single_buffer=False):
    nd = len(shape)
    if single_buffer:
        return pl.BlockSpec(shape, lambda *_: (0,) * nd, pipeline_mode=pl.Buffered(1))
    return pl.BlockSpec(shape, lambda *_: (0,) * nd)


def _in_proj0_kernel(x_ref, g_ref, w_ref, cos_ref, sin_ref, o_ref, svt_ref, dvt_ref):
    h = _rms(x_ref[...], g_ref[...]).astype(BF16)
    tm = h.shape[0]
    lane = lax.broadcasted_iota(jnp.int32, (tm, LANES), 1)
    first_half = (lane % HEAD_DIM) < (HEAD_DIM // 2)
    cos = cos_ref[...]
    sin = sin_ref[...]
    seg = SB_WIDTH
    q_scale = SB_SCALE * LOG2E
    out_col = {0: 0, 1: 1, 3: 2, 4: 3}
    for c in range(EVEN_IN_WIDTH // seg):
        y = _dot(h, w_ref[:, c * seg:(c + 1) * seg])
        for s in range(seg // LANES):
            t = y[:, s * LANES:(s + 1) * LANES]
            if c in (2, 5):
                (svt_ref if c == 2 else dvt_ref)[0, s] = t.T.astype(BF16)
                continue
            if c in (3, 4):
                swapped = jnp.where(first_half,
                                    pltpu.roll(t, LANES - HEAD_DIM // 2, 1),
                                    pltpu.roll(t, HEAD_DIM // 2, 1))
                t = t * cos + swapped * sin
            if c in (0, 3):
                t = t * q_scale
            col = out_col[c] * seg + s * LANES
            o_ref[:, col:col + LANES] = t.astype(BF16)


def _in_proj0(x2, g, w, cos_t, sin_t, batch, seq, tm):
    n = x2.shape[0]
    per_seq = seq // tm
    vt_spec = pl.BlockSpec((1, SB_WIDTH // LANES, LANES, tm),
                           lambda i: (i // per_seq, 0, 0, i % per_seq))
    vt_shape = jax.ShapeDtypeStruct((batch, SB_WIDTH // LANES, LANES, seq), BF16)
    return pl.pallas_call(
        _in_proj0_kernel,
        grid=(n // tm,),
        in_specs=[
            pl.BlockSpec((tm, D_MODEL), lambda i: (i, 0)),
            _const_spec((1, D_MODEL)),
            _const_spec((D_MODEL, EVEN_IN_WIDTH)),
            pl.BlockSpec((tm, LANES), lambda i: (i % per_seq, 0)),
            pl.BlockSpec((tm, LANES), lambda i: (i % per_seq, 0)),
        ],
        out_specs=[pl.BlockSpec((tm, QK_WIDTH), lambda i: (i, 0)), vt_spec, vt_spec],
        out_shape=[jax.ShapeDtypeStruct((n, QK_WIDTH), BF16), vt_shape, vt_shape],
        compiler_params=_params("arbitrary"),
        name="l0_in_proj",
    )(x2, g, w, cos_t, sin_t)


def _sb_kernel(q_ref, k_ref, vt_ref, o_ref, acc_ref, rest_ref, *, tq):
    lane = lax.broadcasted_iota(jnp.int32, (tq, LANES), 1)
    low = lane < HEAD_DIM
    tri_row = lax.broadcasted_iota(jnp.int32, (tq + ONES_ROWS, tq), 0)
    tri_col = lax.broadcasted_iota(jnp.int32, (tq + ONES_ROWS, tq), 1)
    tri = jnp.logical_or(tri_col > tri_row, tri_row >= tq).astype(BF16)
    head0_rows = lax.broadcasted_iota(jnp.int32, (LANES, tq), 0) < HEAD_DIM

    def sweep(q2, first, nblk, diagonal):
        nk = nblk * tq
        start = pl.multiple_of(first * tq, tq)
        z = _dot_nt(k_ref[pl.ds(start, nk), :], q2)
        log_1m = jnp.minimum(-z, 0.0) - jnp.log2(1.0 + jnp.exp2(-jnp.abs(z)))
        if diagonal:
            key = lax.broadcasted_iota(jnp.int32, (nk, 2 * tq), 0) - (nk - tq)
            qry = jnp.bitwise_and(lax.broadcasted_iota(jnp.int32, (nk, 2 * tq), 1), tq - 1)
            strict = key < qry
            log_1m = jnp.where(strict, log_1m, 0.0)
        hi = log_1m.astype(BF16)
        lo = (log_1m - hi.astype(F32)).astype(BF16)
        rest = rest_ref[...]
        log_w = []
        for b in reversed(range(nblk)):
            rows = slice(b * tq, (b + 1) * tq)
            after = _dot(tri, hi[rows]) + _dot(tri, lo[rows])
            log_w.append(z[rows] + log_1m[rows] + after[:tq] + rest)
            rest = rest + after[tq:tq + 1]
        w = jnp.exp2(jnp.concatenate(log_w[::-1], axis=0))
        if diagonal:
            w = jnp.where(strict, w, 0.0)
        acc_ref[...] += _dot(vt_ref[:, pl.ds(start, nk)], w.astype(BF16))
        rest_ref[...] = rest

    def query_block(i, first_nblk):
        rows = pl.ds(pl.multiple_of(i * tq, tq), tq)
        q = q_ref[rows, :]
        zero = jnp.zeros_like(q)
        q2 = jnp.concatenate([jnp.where(low, q, zero), jnp.where(low, zero, q)], axis=0)
        acc_ref[...] = jnp.zeros_like(acc_ref)
        rest_ref[...] = jnp.zeros_like(rest_ref)
        sweep(q2, i + 1 - first_nblk, first_nblk, True)

        def cond(st):
            j, worst = st
            return jnp.logical_and(j >= 0, worst > SB_LOG2_ZERO)

        def body(st):
            j, _ = st
            sweep(q2, j, 1, False)
            return j - 1, jnp.max(rest_ref[...])

        lax.while_loop(cond, body, (i - first_nblk, jnp.max(rest_ref[...])))
        acc = acc_ref[...]
        o_ref[rows, :] = jnp.where(head0_rows, acc[:, :tq], acc[:, tq:]).T.astype(o_ref.dtype)

    query_block(0, 1)

    def step(i, carry):
        query_block(i, 2)
        return carry

    lax.fori_loop(1, q_ref.shape[0] // tq, step, 0)


def _sb_attention(proj, vt, batch, seq, tq):
    n = proj.shape[0]
    k_off = SB_WIDTH // LANES
    return pl.pallas_call(
        functools.partial(_sb_kernel, tq=tq),
        grid=(batch, SB_WIDTH // LANES),
        in_specs=[
            pl.BlockSpec((seq, LANES), lambda b, p: (b, p)),
            pl.BlockSpec((seq, LANES), lambda b, p: (b, k_off + p)),
            pl.BlockSpec((None, None, LANES, seq), lambda b, p: (b, p, 0, 0)),
        ],
        out_specs=pl.BlockSpec((seq, LANES), lambda b, p: (b, p)),
        out_shape=jax.ShapeDtypeStruct((n, SB_WIDTH), BF16),
        scratch_shapes=[pltpu.VMEM((LANES, 2 * tq), F32), pltpu.VMEM((1, 2 * tq), F32)],
        compiler_params=_params("arbitrary", "arbitrary"),
        name="sb_attention",
    )(proj, proj, vt)


def _diff_kernel(lq1_ref, lk1_ref, lq2_ref, lk2_ref, g_ref, q_ref, k_ref, vt_ref, o_ref,
                 m_ref, acc_ref, s_ref, bmax_ref, *, tq, lambda_init):
    lane = lax.broadcasted_iota(jnp.int32, (tq, LANES), 1)
    low = lane < HEAD_DIM
    ones = jnp.ones((ONES_ROWS, tq), BF16)
    lam = (jnp.exp(jnp.sum(lq1_ref[...] * lk1_ref[...], axis=1, keepdims=True))
           - jnp.exp(jnp.sum(lq2_ref[...] * lk2_ref[...], axis=1, keepdims=True))
           + lambda_init)

    def query_block(i, carry):
        rows = pl.ds(pl.multiple_of(i * tq, tq), tq)
        q = q_ref[rows, :]
        zero = jnp.zeros_like(q)
        q_maps = (jnp.where(low, q, zero), jnp.where(low, zero, q))
        m_ref[...] = jnp.full_like(m_ref, NEG)
        acc_ref[...] = jnp.zeros_like(acc_ref)

        def scores(j, buf, diagonal=False):
            k = k_ref[pl.ds(pl.multiple_of(j * tq, tq), tq), :]
            for mp in range(2):
                s = _dot_nt(k, q_maps[mp])
                if diagonal:
                    visible = ((lax.broadcasted_iota(jnp.int32, (tq, tq), 0) // CHUNK)
                               <= (lax.broadcasted_iota(jnp.int32, (tq, tq), 1) // CHUNK))
                    s = jnp.where(visible, s, NEG)
                s_ref[buf, mp] = s
                bmax_ref[buf, mp] = jnp.max(s, axis=0, keepdims=True)

        def accumulate(j, buf):
            vt = jnp.concatenate([vt_ref[:, pl.ds(pl.multiple_of(j * tq, tq), tq)], ones],
                                 axis=0)
            for mp in range(2):
                m_old = m_ref[mp]
                m_new = jnp.maximum(m_old, bmax_ref[buf, mp])
                p = jnp.exp2(s_ref[buf, mp] - m_new).astype(BF16)
                acc_ref[mp] = jnp.exp2(m_old - m_new) * acc_ref[mp] + _dot(vt, p)
                m_ref[mp] = m_new

        scores(i, 0, diagonal=True)
        pairs = i // 2

        def body(jj, c):
            scores(2 * jj, 1)
            accumulate(jnp.where(jj == 0, i, 2 * jj - 1), 0)
            scores(2 * jj + 1, 0)
            accumulate(2 * jj, 1)
            return c

        lax.fori_loop(0, pairs, body, 0)
        held = jnp.where(pairs == 0, i, 2 * pairs - 1)

        @pl.when(i % 2 == 1)
        def _():
            scores(i - 1, 1)
            accumulate(held, 0)
            accumulate(i - 1, 1)

        @pl.when(i % 2 == 0)
        def _():
            accumulate(held, 0)

        acc1 = acc_ref[0]
        acc2 = acc_ref[1]
        out_t = (acc1[:LANES] * (1.0 / acc1[LANES:LANES + 1])
                 - lam * (acc2[:LANES] * (1.0 / acc2[LANES:LANES + 1])))
        out = _rms(out_t.T, g_ref[...]) * (1.0 - lambda_init)
        o_ref[rows, :] = out.astype(o_ref.dtype)
        return carry

    lax.fori_loop(0, q_ref.shape[0] // tq, query_block, 0)


def _diff_attention(proj, vt, lq1, lk1, lq2, lk2, subln_g, batch, seq, tq, lambda_init):
    n = proj.shape[0]
    q_off = 2 * SB_WIDTH // LANES
    k_off = q_off + DIFF_WIDTH // LANES
    vec = _const_spec((1, HEAD_DIM))
    return pl.pallas_call(
        functools.partial(_diff_kernel, tq=tq, lambda_init=lambda_init),
        grid=(batch, DIFF_HEADS),
        in_specs=[
            vec, vec, vec, vec,
            _const_spec((1, LANES)),
            pl.BlockSpec((seq, LANES), lambda b, h: (b, q_off + h)),
            pl.BlockSpec((seq, LANES), lambda b, h: (b, k_off + h)),
            pl.BlockSpec((None, None, LANES, seq), lambda b, h: (b, h, 0, 0)),
        ],
        out_specs=pl.BlockSpec((seq, LANES), lambda b, h: (b, h)),
        out_shape=jax.ShapeDtypeStruct((n, DIFF_WIDTH), BF16),
        scratch_shapes=[pltpu.VMEM((2, 1, tq), F32),
                        pltpu.VMEM((2, LANES + ONES_ROWS, tq), F32),
                        pltpu.VMEM((2, 2, tq, tq), F32),
                        pltpu.VMEM((2, 2, 1, tq), F32)],
        compiler_params=_params("arbitrary", "arbitrary"),
        name="diff_attention",
    )(lq1, lk1, lq2, lk2, subln_g, proj, proj, vt)


def _out_ffn_kernel(x_ref, a_ref, b_ref, wo_ref, gf_ref, wg_ref, wu_ref, wd_ref, gn_ref, o_ref,
                    acc_ref, *, final_norm):
    half = a_ref.shape[1]
    x1 = x_ref[...] + _dot(a_ref[...], wo_ref[:half, :]) + _dot(b_ref[...], wo_ref[half:, :])
    h = _rms(x1, gf_ref[...]).astype(BF16)
    acc_ref[...] = x1

    def body(c, carry):
        cols = pl.ds(pl.multiple_of(c * FF_CHUNK, FF_CHUNK), FF_CHUNK)
        gate = _dot(h, wg_ref[:, cols])
        act = (gate * _sigmoid(gate) * _dot(h, wu_ref[:, cols])).astype(BF16)
        acc_ref[...] += _dot(act, wd_ref[cols, :])
        return carry

    lax.fori_loop(0, D_FF // FF_CHUNK, body, 0)
    y = acc_ref[...]
    if final_norm:
        y = _rms(y, gn_ref[...])
    o_ref[...] = y


def _out_ffn(x2, a, b, wo, gf, wg, wu, wd, gn, tm, final_norm):
    n = x2.shape[0]
    half = a.shape[1]
    return pl.pallas_call(
        functools.partial(_out_ffn_kernel, final_norm=final_norm),
        grid=(n // tm,),
        in_specs=[
            pl.BlockSpec((tm, D_MODEL), lambda i: (i, 0)),
            pl.BlockSpec((tm, half), lambda i: (i, 0)),
            pl.BlockSpec((tm, half), lambda i: (i, 0)),
            _const_spec((2 * half, D_MODEL), single_buffer=True),
            _const_spec((1, D_MODEL)),
            _const_spec((D_MODEL, D_FF), single_buffer=True),
            _const_spec((D_MODEL, D_FF), single_buffer=True),
            _const_spec((D_FF, D_MODEL), single_buffer=True),
            _const_spec((1, D_MODEL)),
        ],
        out_specs=pl.BlockSpec((tm, D_MODEL), lambda i: (i, 0)),
        out_shape=jax.ShapeDtypeStruct((n, D_MODEL), F32),
        scratch_shapes=[pltpu.VMEM((tm, D_MODEL), F32)],
        compiler_params=_params("arbitrary"),
        name="out_proj_ffn",
    )(x2, a, b, wo, gf, wg, wu, wd, gn)


def _l1_mixer_kernel(x_ref, g_ref, w_ref, pw_ref, ps_ref, dw_ref, db_ref, cg_ref, cb_ref,
                     c_ref, d_ref, ext_p, ext_u, shift_ref, conv_ref):
    ti = pl.program_id(1)
    tm = x_ref.shape[0]

    @pl.when(ti == 0)
    def _():
        ext_p[:HALO, :] = jnp.zeros((HALO, POOL_WIDTH), F32)
        ext_u[:HALO, :] = jnp.zeros((HALO, CONV_WIDTH), F32)

    h = _rms(x_ref[...], g_ref[...]).astype(BF16)
    xp = _dot(h, w_ref[:, :POOL_WIDTH])
    xa = _dot(h, w_ref[:, POOL_WIDTH:POOL_WIDTH + CONV_WIDTH])
    xg = _dot(h, w_ref[:, POOL_WIDTH + CONV_WIDTH:])
    ext_p[HALO:, :] = xp
    ext_u[HALO:, :] = xa * _sigmoid(xg)

    pos1 = (ti * tm + 1 + lax.broadcasted_iota(jnp.int32, (tm, 1), 0)).astype(F32)
    for g, win in enumerate(POOL_WINDOWS):
        cols = slice(g * POOL_GROUP_DIM, (g + 1) * POOL_GROUP_DIM)
        total = ext_p[HALO:, cols]
        for back in range(1, win):
            total = total + ext_p[HALO - back:HALO - back + tm, cols]
        pooled = total / jnp.minimum(pos1, float(win)) - ext_p[HALO:, cols]
        mixed = _dot(pooled.astype(BF16), pw_ref[g])
        c_ref[:, cols] = (mixed * ps_ref[:, cols]).astype(c_ref.dtype)

    for b in range(1, SUBLANES):
        shift_ref[b - 1, SUBLANES:, :] = ext_u[SUBLANES - b:HALO + tm - b, :]

    def row_block(rb, carry):
        r0 = pl.multiple_of(rb * CONV_ROWS, CONV_ROWS)
        conv = jnp.zeros((CONV_ROWS, CONV_WIDTH), F32) + db_ref[...]
        for back in range(CONV_KERNEL):
            a, b = divmod(back, SUBLANES)
            src = ext_u if b == 0 else shift_ref.at[b - 1]
            tap = CONV_KERNEL - 1 - back
            conv = conv + (src[pl.ds(r0 + HALO - SUBLANES * a, CONV_ROWS), :]
                           * jnp.concatenate([dw_ref[tap]] * (CONV_ROWS // SUBLANES), axis=0))
        conv_ref[pl.ds(r0, CONV_ROWS), :] = conv
        return carry

    lax.fori_loop(0, tm // CONV_ROWS, row_block, 0)
    conv = conv_ref[...]
    mu = jnp.mean(conv, axis=-1, keepdims=True)
    cen = conv - mu
    var = jnp.mean(cen * cen, axis=-1, keepdims=True)
    y = cen * lax.rsqrt(var + EPS) * cg_ref[...] + cb_ref[...]
    d_ref[...] = (y * _sigmoid(y)).astype(d_ref.dtype)

    ext_p[:HALO, :] = ext_p[tm:, :]
    ext_u[:HALO, :] = ext_u[tm:, :]


def _l1_mixer(x2, g, w, pw, ps, dw, db, cg, cb, batch, seq, tm):
    n = x2.shape[0]
    nt = seq // tm
    row = lambda b, t: (b * nt + t, 0)
    return pl.pallas_call(
        _l1_mixer_kernel,
        grid=(batch, nt),
        in_specs=[
            pl.BlockSpec((tm, D_MODEL), row),
            _const_spec((1, D_MODEL)),
            _const_spec((D_MODEL, ODD_IN_WIDTH)),
            _const_spec((len(POOL_WINDOWS), POOL_GROUP_DIM, POOL_GROUP_DIM)),
            _const_spec((1, POOL_WIDTH)),
            _const_spec((CONV_KERNEL, SUBLANES, CONV_WIDTH)),
            _const_spec((1, CONV_WIDTH)),
            _const_spec((1, CONV_WIDTH)),
            _const_spec((1, CONV_WIDTH)),
        ],
        out_specs=[pl.BlockSpec((tm, POOL_WIDTH), row), pl.BlockSpec((tm, CONV_WIDTH), row)],
        out_shape=[jax.ShapeDtypeStruct((n, POOL_WIDTH), BF16),
                   jax.ShapeDtypeStruct((n, CONV_WIDTH), BF16)],
        scratch_shapes=[pltpu.VMEM((HALO + tm, POOL_WIDTH), F32),
                        pltpu.VMEM((HALO + tm, CONV_WIDTH), F32),
                        pltpu.VMEM((SUBLANES - 1, HALO + tm, CONV_WIDTH), F32),
                        pltpu.VMEM((tm, CONV_WIDTH), F32)],
        compiler_params=_params("arbitrary", "arbitrary"),
        name="l1_mixer",
    )(x2, g, w, pw, ps, dw, db, cg, cb)


def _rope_tables(seq):
    half = HEAD_DIM // 2
    inv = jnp.power(ROPE_THETA, -jnp.arange(0, HEAD_DIM, 2, dtype=F32) / HEAD_DIM)
    ang = jnp.arange(seq, dtype=jnp.int32).astype(F32)[:, None] * inv[None, :]
    cos = jnp.cos(ang)
    sin = jnp.sin(ang)
    reps = LANES // HEAD_DIM
    return (jnp.tile(jnp.concatenate([cos, cos], axis=1), (1, reps)),
            jnp.tile(jnp.concatenate([-sin, sin], axis=1), (1, reps)))


def _tile(seq, want):
    return want if seq % want == 0 else seq


def kernel(x, mix_norm_0, w_in_0, lambda_q1_0, lambda_k1_0, lambda_q2_0, lambda_k2_0, subln_0, w_out_0, ffn_norm_0, w_gate_0, w_up_0, w_down_0, mix_norm_1, w_in_1, pool_w_1, pool_scale_1, dw_w_1, dw_b_1, conv_norm_g_1, conv_norm_b_1, w_out_1, ffn_norm_1, w_gate_1, w_up_1, w_down_1, final_norm):
    batch, seq, d = x.shape
    assert d == D_MODEL and seq % LANES == 0
    x2 = x.reshape(batch * seq, d)
    row = lambda v: v.reshape(1, -1).astype(F32)
    tm = _tile(seq, 512)
    tm_ffn = _tile(seq, 1024)

    cos_t, sin_t = _rope_tables(seq)
    proj, svt, dvt = _in_proj0(x2, row(mix_norm_0), w_in_0.astype(BF16), cos_t, sin_t,
                               batch, seq, tm)
    a_out = _sb_attention(proj, svt, batch, seq, _tile(seq, 256))
    lambda_init = 0.8 - 0.6 * math.exp(-0.3 * 0)
    b_out = _diff_attention(proj, dvt, row(lambda_q1_0), row(lambda_k1_0), row(lambda_q2_0),
                            row(lambda_k2_0), row(subln_0), batch, seq, _tile(seq, 512),
                            lambda_init)
    x2 = _out_ffn(x2, a_out, b_out, w_out_0.astype(BF16), row(ffn_norm_0),
                  w_gate_0.astype(BF16), w_up_0.astype(BF16), w_down_0.astype(BF16),
                  row(final_norm), tm_ffn, final_norm=False)

    c_out, d_out = _l1_mixer(x2, row(mix_norm_1), w_in_1.astype(BF16), pool_w_1.astype(BF16),
                             row(pool_scale_1),
                             jnp.broadcast_to(dw_w_1.astype(F32)[:, None, :],
                                              (CONV_KERNEL, SUBLANES, CONV_WIDTH)),
                             row(dw_b_1),
                             row(conv_norm_g_1), row(conv_norm_b_1), batch, seq, tm)
    out = _out_ffn(x2, c_out, d_out, w_out_1.astype(BF16), row(ffn_norm_1),
                   w_gate_1.astype(BF16), w_up_1.astype(BF16), w_down_1.astype(BF16),
                   row(final_norm), tm_ffn, final_norm=True)
    return out.reshape(batch, seq, d)
```

```python
import functools
import math

import jax
import jax.numpy as jnp
from jax import lax
from jax.experimental import pallas as pl
from jax.experimental.pallas import tpu as pltpu

F32 = jnp.float32
BF16 = jnp.bfloat16

D_MODEL = 1024
HEAD_DIM = 64
LANES = 128
SUBLANES = 8
CONV_ROWS = 64
SB_HEADS = 8
DIFF_HEADS = 4
SB_WIDTH = SB_HEADS * HEAD_DIM
DIFF_WIDTH = DIFF_HEADS * 2 * HEAD_DIM
EVEN_IN_WIDTH = 3 * SB_WIDTH + 3 * DIFF_WIDTH
CHUNK = 64
POOL_WINDOWS = (2, 4, 8, 16)
POOL_WIDTH = 512
POOL_GROUP_DIM = 128
CONV_WIDTH = 512
CONV_KERNEL = 31
ODD_IN_WIDTH = POOL_WIDTH + 2 * CONV_WIDTH
D_FF = 2816
FF_CHUNK = 256
ROPE_THETA = 10000.0
EPS = 1e-6
NEG = -1e30
SB_SCALE = HEAD_DIM ** -0.5
LOG2E = math.log2(math.e)
ONES_ROWS = 16
HALO = 32
QK_WIDTH = 2 * SB_WIDTH + 2 * DIFF_WIDTH
SB_LOG2_ZERO = -150.0 * math.log2(math.e)
F32_EXP2_MAX = 126.0
VMEM_LIMIT = 56 * 1024 * 1024


def _rms(x, g):
    return x * lax.rsqrt(jnp.mean(x * x, axis=-1, keepdims=True) + EPS) * g


def _sigmoid(x):
    return 1.0 / (1.0 + jnp.exp(-x))


def _dot(a, b):
    return jnp.dot(a, b, preferred_element_type=F32)


def _dot_nt(a, b):
    return lax.dot_general(a, b, (((1,), (1,)), ((), ())), preferred_element_type=F32)


def _params(*sem):
    return pltpu.CompilerParams(dimension_semantics=sem, vmem_limit_bytes=VMEM_LIMIT)


def _const_spec(shape, single_buffer=False):
    nd = len(shape)
    if single_buffer:
        return pl.BlockSpec(shape, lambda *_: (0,) * nd, pipeline_mode=pl.Buffered(1))
    return pl.BlockSpec(shape, lambda *_: (0,) * nd)


def _in_proj0_kernel(x_ref, g_ref, w_ref, cos_ref, sin_ref, o_ref, svt_ref, dvt_ref):
    h = _rms(x_ref[...], g_ref[...]).astype(BF16)
    tm = h.shape[0]
    lane = lax.broadcasted_iota(jnp.int32, (tm, LANES), 1)
    first_half = (lane % HEAD_DIM) < (HEAD_DIM // 2)
    cos = cos_ref[...]
    sin = sin_ref[...]
    seg = SB_WIDTH
    q_scale = SB_SCALE * LOG2E
    out_col = {0: 0, 1: 1, 3: 2, 4: 3}
    for c in range(EVEN_IN_WIDTH // seg):
        y = _dot(h, w_ref[:, c * seg:(c + 1) * seg])
        for s in range(seg // LANES):
            t = y[:, s * LANES:(s + 1) * LANES]
            if c in (2, 5):
                (svt_ref if c == 2 else dvt_ref)[0, s] = t.T.astype(BF16)
                continue
            if c in (3, 4):
                swapped = jnp.where(first_half,
                                    pltpu.roll(t, LANES - HEAD_DIM // 2, 1),
                                    pltpu.roll(t, HEAD_DIM // 2, 1))
                t = t * cos + swapped * sin
            if c in (0, 3):
                t = t * q_scale
            col = out_col[c] * seg + s * LANES
            o_ref[:, col:col + LANES] = t.astype(BF16)


def _in_proj0(x2, g, w, cos_t, sin_t, batch, seq, tm):
    n = x2.shape[0]
    per_seq = seq // tm
    vt_spec = pl.BlockSpec((1, SB_WIDTH // LANES, LANES, tm),
                           lambda i: (i // per_seq, 0, 0, i % per_seq))
    vt_shape = jax.ShapeDtypeStruct((batch, SB_WIDTH // LANES, LANES, seq), BF16)
    return pl.pallas_call(
        _in_proj0_kernel,
        grid=(n // tm,),
        in_specs=[
            pl.BlockSpec((tm, D_MODEL), lambda i: (i, 0)),
            _const_spec((1, D_MODEL)),
            _const_spec((D_MODEL, EVEN_IN_WIDTH)),
            pl.BlockSpec((tm, LANES), lambda i: (i % per_seq, 0)),
            pl.BlockSpec((tm, LANES), lambda i: (i % per_seq, 0)),
        ],
        out_specs=[pl.BlockSpec((tm, QK_WIDTH), lambda i: (i, 0)), vt_spec, vt_spec],
        out_shape=[jax.ShapeDtypeStruct((n, QK_WIDTH), BF16), vt_shape, vt_shape],
        compiler_params=_params("arbitrary"),
        name="l0_in_proj",
    )(x2, g, w, cos_t, sin_t)


def _log2_one_minus_beta(z):
    return -jnp.maximum(z, jnp.log2(1.0 + jnp.exp2(jnp.minimum(z, F32_EXP2_MAX))))


def _sb_kernel(q_ref, k_ref, vt_ref, o_ref, acc_ref, rest_ref, zl_ref, hi_ref, lo_ref, *, tq):
    lane = lax.broadcasted_iota(jnp.int32, (tq, LANES), 1)
    low = lane < HEAD_DIM
    tri_row = lax.broadcasted_iota(jnp.int32, (tq + ONES_ROWS, tq), 0)
    tri_col = lax.broadcasted_iota(jnp.int32, (tq + ONES_ROWS, tq), 1)
    tri = jnp.logical_or(tri_col > tri_row, tri_row >= tq).astype(BF16)
    head0_rows = lax.broadcasted_iota(jnp.int32, (LANES, tq), 0) < HEAD_DIM

    def sweep(q2, first, nblk, diagonal):
        nk = nblk * tq
        start = pl.multiple_of(first * tq, tq)
        z = _dot_nt(k_ref[pl.ds(start, nk), :], q2)
        log_1m = _log2_one_minus_beta(z)
        if diagonal:
            key = lax.broadcasted_iota(jnp.int32, (nk, 2 * tq), 0) - (nk - tq)
            qry = jnp.bitwise_and(lax.broadcasted_iota(jnp.int32, (nk, 2 * tq), 1), tq - 1)
            strict = key < qry
            log_1m = jnp.where(strict, log_1m, 0.0)
        hi = log_1m.astype(BF16)
        lo = (log_1m - hi.astype(F32)).astype(BF16)
        rest = rest_ref[...]
        log_w = []
        for b in reversed(range(nblk)):
            rows = slice(b * tq, (b + 1) * tq)
            after = _dot(tri, hi[rows]) + _dot(tri, lo[rows])
            log_w.append(z[rows] + log_1m[rows] + after[:tq] + rest)
            rest = rest + after[tq:tq + 1]
        w = jnp.exp2(jnp.concatenate(log_w[::-1], axis=0))
        if diagonal:
            w = jnp.where(strict, w, 0.0)
        acc_ref[...] += _dot(vt_ref[:, pl.ds(start, nk)], w.astype(BF16))
        rest_ref[...] = rest

    def masked_queries(i):
        q = q_ref[pl.ds(pl.multiple_of(i * tq, tq), tq), :]
        zero = jnp.zeros_like(q)
        return jnp.concatenate([jnp.where(low, q, zero), jnp.where(low, zero, q)], axis=0)

    def finish(i, next_block):
        def cond(st):
            j, worst = st
            return jnp.logical_and(j >= 0, worst > SB_LOG2_ZERO)

        def body(st):
            j, _ = st
            sweep(masked_queries(i), j, 1, False)
            return j - 1, jnp.max(rest_ref[...])

        lax.while_loop(cond, body, (next_block, jnp.max(rest_ref[...])))
        acc = acc_ref[...]
        o_ref[pl.ds(pl.multiple_of(i * tq, tq), tq), :] = jnp.where(
            head0_rows, acc[:, :tq], acc[:, tq:]).T.astype(o_ref.dtype)

    def stage_a(i, buf):
        start = pl.multiple_of((i - 1) * tq, tq)
        z = _dot_nt(k_ref[pl.ds(start, 2 * tq), :], masked_queries(i))
        log_1m = _log2_one_minus_beta(z)
        key = lax.broadcasted_iota(jnp.int32, (tq, 2 * tq), 0)
        qry = jnp.bitwise_and(lax.broadcasted_iota(jnp.int32, (tq, 2 * tq), 1), tq - 1)
        strict = key < qry
        for b in range(2):
            rows = slice(b * tq, (b + 1) * tq)
            lm = log_1m[rows]
            zl = z[rows] + lm
            if b == 1:
                lm = jnp.where(strict, lm, 0.0)
                zl = jnp.where(strict, zl, NEG)
            hi = lm.astype(BF16)
            hi_ref[buf, rows, :] = hi
            lo_ref[buf, rows, :] = (lm - hi.astype(F32)).astype(BF16)
            zl_ref[buf, rows, :] = zl

    def stage_b(i, buf):
        rest = jnp.zeros((1, 2 * tq), F32)
        log_w = []
        for b in (1, 0):
            rows = slice(b * tq, (b + 1) * tq)
            after = _dot(tri, hi_ref[buf, rows, :]) + _dot(tri, lo_ref[buf, rows, :])
            log_w.append(zl_ref[buf, rows, :] + after[:tq] + rest)
            rest = rest + after[tq:tq + 1]
        w = jnp.exp2(jnp.concatenate(log_w[::-1], axis=0)).astype(BF16)
        start = pl.multiple_of((i - 1) * tq, tq)
        acc_ref[...] = _dot(vt_ref[:, pl.ds(start, 2 * tq)], w)
        rest_ref[...] = rest
        finish(i, i - 2)

    acc_ref[...] = jnp.zeros_like(acc_ref)
    rest_ref[...] = jnp.zeros_like(rest_ref)
    sweep(masked_queries(0), 0, 1, True)
    finish(0, -1)

    nq = q_ref.shape[0] // tq
    if nq > 1:
        stage_a(1, 1)

        def body(t, carry):
            i = 2 * t + 1
            stage_a(i + 1, 0)
            stage_b(i, 1)
            stage_a(i + 2, 1)
            stage_b(i + 1, 0)
            return carry

        lax.fori_loop(0, nq // 2 - 1, body, 0)
        stage_b(nq - 1, 1)


def _sb_attention(proj, vt, batch, seq, tq):
    n = proj.shape[0]
    nq = seq // tq
    assert tq & (tq - 1) == 0 and (nq == 1 or nq % 2 == 0)
    k_off = SB_WIDTH // LANES
    return pl.pallas_call(
        functools.partial(_sb_kernel, tq=tq),
        grid=(batch, SB_WIDTH // LANES),
        in_specs=[
            pl.BlockSpec((seq, LANES), lambda b, p: (b, p)),
            pl.BlockSpec((seq, LANES), lambda b, p: (b, k_off + p)),
            pl.BlockSpec((None, None, LANES, seq), lambda b, p: (b, p, 0, 0)),
        ],
        out_specs=pl.BlockSpec((seq, LANES), lambda b, p: (b, p)),
        out_shape=jax.ShapeDtypeStruct((n, SB_WIDTH), BF16),
        scratch_shapes=[pltpu.VMEM((LANES, 2 * tq), F32), pltpu.VMEM((1, 2 * tq), F32),
                        pltpu.VMEM((2, 2 * tq, 2 * tq), F32),
                        pltpu.VMEM((2, 2 * tq, 2 * tq), BF16),
                        pltpu.VMEM((2, 2 * tq, 2 * tq), BF16)],
        compiler_params=_params("arbitrary", "arbitrary"),
        name="sb_attention",
    )(proj, proj, vt)


def _diff_kernel(lq1_ref, lk1_ref, lq2_ref, lk2_ref, g_ref, q_ref, k_ref, vt_ref, o_ref,
                 m_ref, acc_ref, s_ref, bmax_ref, *, tq, lambda_init):
    lane = lax.broadcasted_iota(jnp.int32, (tq, LANES), 1)
    low = lane < HEAD_DIM
    ones = jnp.ones((ONES_ROWS, tq), BF16)
    lam = (jnp.exp(jnp.sum(lq1_ref[...] * lk1_ref[...], axis=1, keepdims=True))
           - jnp.exp(jnp.sum(lq2_ref[...] * lk2_ref[...], axis=1, keepdims=True))
           + lambda_init)

    def query_block(i, carry):
        rows = pl.ds(pl.multiple_of(i * tq, tq), tq)
        q = q_ref[rows, :]
        zero = jnp.zeros_like(q)
        q_maps = (jnp.where(low, q, zero), jnp.where(low, zero, q))
        m_ref[...] = jnp.full_like(m_ref, NEG)
        acc_ref[...] = jnp.zeros_like(acc_ref)

        def scores(j, buf, diagonal=False):
            k = k_ref[pl.ds(pl.multiple_of(j * tq, tq), tq), :]
            for mp in range(2):
                s = _dot_nt(k, q_maps[mp])
                if diagonal:
                    visible = ((lax.broadcasted_iota(jnp.int32, (tq, tq), 0) // CHUNK)
                               <= (lax.broadcasted_iota(jnp.int32, (tq, tq), 1) // CHUNK))
                    s = jnp.where(visible, s, NEG)
                s_ref[buf, mp] = s
                bmax_ref[buf, mp] = jnp.max(s, axis=0, keepdims=True)

        def accumulate(j, buf):
            vt = jnp.concatenate([vt_ref[:, pl.ds(pl.multiple_of(j * tq, tq), tq)], ones],
                                 axis=0)
            for mp in range(2):
                m_old = m_ref[mp]
                m_new = jnp.maximum(m_old, bmax_ref[buf, mp])
                p = jnp.exp2(s_ref[buf, mp] - m_new).astype(BF16)
                acc_ref[mp] = jnp.exp2(m_old - m_new) * acc_ref[mp] + _dot(vt, p)
                m_ref[mp] = m_new

        scores(i, 0, diagonal=True)
        pairs = i // 2

        def body(jj, c):
            scores(2 * jj, 1)
            accumulate(jnp.where(jj == 0, i, 2 * jj - 1), 0)
            scores(2 * jj + 1, 0)
            accumulate(2 * jj, 1)
            return c

        lax.fori_loop(0, pairs, body, 0)
        held = jnp.where(pairs == 0, i, 2 * pairs - 1)

        @pl.when(i % 2 == 1)
        def _():
            scores(i - 1, 1)
            accumulate(held, 0)
            accumulate(i - 1, 1)

        @pl.when(i % 2 == 0)
        def _():
            accumulate(held, 0)

        acc1 = acc_ref[0]
        acc2 = acc_ref[1]
        out_t = (acc1[:LANES] * (1.0 / acc1[LANES:LANES + 1])
                 - lam * (acc2[:LANES] * (1.0 / acc2[LANES:LANES + 1])))
        out = _rms(out_t.T, g_ref[...]) * (1.0 - lambda_init)
        o_ref[rows, :] = out.astype(o_ref.dtype)
        return carry

    lax.fori_loop(0, q_ref.shape[0] // tq, query_block, 0)


def _diff_attention(proj, vt, lq1, lk1, lq2, lk2, subln_g, batch, seq, tq, lambda_init):
    n = proj.shape[0]
    q_off = 2 * SB_WIDTH // LANES
    k_off = q_off + DIFF_WIDTH // LANES
    vec = _const_spec((1, HEAD_DIM))
    return pl.pallas_call(
        functools.partial(_diff_kernel, tq=tq, lambda_init=lambda_init),
        grid=(batch, DIFF_HEADS),
        in_specs=[
            vec, vec, vec, vec,
            _const_spec((1, LANES)),
            pl.BlockSpec((seq, LANES), lambda b, h: (b, q_off + h)),
            pl.BlockSpec((seq, LANES), lambda b, h: (b, k_off + h)),
            pl.BlockSpec((None, None, LANES, seq), lambda b, h: (b, h, 0, 0)),
        ],
        out_specs=pl.BlockSpec((seq, LANES), lambda b, h: (b, h)),
        out_shape=jax.ShapeDtypeStruct((n, DIFF_WIDTH), BF16),
        scratch_shapes=[pltpu.VMEM((2, 1, tq), F32),
                        pltpu.VMEM((2, LANES + ONES_ROWS, tq), F32),
                        pltpu.VMEM((2, 2, tq, tq), F32),
                        pltpu.VMEM((2, 2, 1, tq), F32)],
        compiler_params=_params("arbitrary", "arbitrary"),
        name="diff_attention",
    )(lq1, lk1, lq2, lk2, subln_g, proj, proj, vt)


def _out_ffn_kernel(x_ref, a_ref, b_ref, wo_ref, gf_ref, wg_ref, wu_ref, wd_ref, gn_ref, o_ref,
                    acc_ref, *, final_norm):
    half = a_ref.shape[1]
    x1 = x_ref[...] + _dot(a_ref[...], wo_ref[:half, :]) + _dot(b_ref[...], wo_ref[half:, :])
    h = _rms(x1, gf_ref[...]).astype(BF16)
    acc_ref[...] = x1

    def body(c, carry):
        cols = pl.ds(pl.multiple_of(c * FF_CHUNK, FF_CHUNK), FF_CHUNK)
        gate = _dot(h, wg_ref[:, cols])
        act = (gate * _sigmoid(gate) * _dot(h, wu_ref[:, cols])).astype(BF16)
        acc_ref[...] += _dot(act, wd_ref[cols, :])
        return carry

    lax.fori_loop(0, D_FF // FF_CHUNK, body, 0)
    y = acc_ref[...]
    if final_norm:
        y = _rms(y, gn_ref[...])
    o_ref[...] = y


def _out_ffn(x2, a, b, wo, gf, wg, wu, wd, gn, tm, final_norm):
    n = x2.shape[0]
    half = a.shape[1]
    return pl.pallas_call(
        functools.partial(_out_ffn_kernel, final_norm=final_norm),
        grid=(n // tm,),
        in_specs=[
            pl.BlockSpec((tm, D_MODEL), lambda i: (i, 0)),
            pl.BlockSpec((tm, half), lambda i: (i, 0)),
            pl.BlockSpec((tm, half), lambda i: (i, 0)),
            _const_spec((2 * half, D_MODEL), single_buffer=True),
            _const_spec((1, D_MODEL)),
            _const_spec((D_MODEL, D_FF), single_buffer=True),
            _const_spec((D_MODEL, D_FF), single_buffer=True),
            _const_spec((D_FF, D_MODEL), single_buffer=True),
            _const_spec((1, D_MODEL)),
        ],
        out_specs=pl.BlockSpec((tm, D_MODEL), lambda i: (i, 0)),
        out_shape=jax.ShapeDtypeStruct((n, D_MODEL), F32),
        scratch_shapes=[pltpu.VMEM((tm, D_MODEL), F32)],
        compiler_params=_params("arbitrary"),
        name="out_proj_ffn",
    )(x2, a, b, wo, gf, wg, wu, wd, gn)


def _l1_mixer_kernel(x_ref, g_ref, w_ref, pw_ref, ps_ref, dw_ref, db_ref, cg_ref, cb_ref,
                     c_ref, d_ref, ext_p, ext_u, shift_ref, conv_ref):
    ti = pl.program_id(1)
    tm = x_ref.shape[0]

    @pl.when(ti == 0)
    def _():
        ext_p[:HALO, :] = jnp.zeros((HALO, POOL_WIDTH), F32)
        ext_u[:HALO, :] = jnp.zeros((HALO, CONV_WIDTH), F32)

    h = _rms(x_ref[...], g_ref[...]).astype(BF16)
    xp = _dot(h, w_ref[:, :POOL_WIDTH])
    xa = _dot(h, w_ref[:, POOL_WIDTH:POOL_WIDTH + CONV_WIDTH])
    xg = _dot(h, w_ref[:, POOL_WIDTH + CONV_WIDTH:])
    ext_p[HALO:, :] = xp
    ext_u[HALO:, :] = xa * _sigmoid(xg)

    pos1 = (ti * tm + 1 + lax.broadcasted_iota(jnp.int32, (tm, 1), 0)).astype(F32)
    for g, win in enumerate(POOL_WINDOWS):
        cols = slice(g * POOL_GROUP_DIM, (g + 1) * POOL_GROUP_DIM)
        total = ext_p[HALO:, cols]
        for back in range(1, win):
            total = total + ext_p[HALO - back:HALO - back + tm, cols]
        pooled = total / jnp.minimum(pos1, float(win)) - ext_p[HALO:, cols]
        mixed = _dot(pooled.astype(BF16), pw_ref[g])
        c_ref[:, cols] = (mixed * ps_ref[:, cols]).astype(c_ref.dtype)

    for b in range(1, SUBLANES):
        shift_ref[b - 1, SUBLANES:, :] = ext_u[SUBLANES - b:HALO + tm - b, :]

    def row_block(rb, carry):
        r0 = pl.multiple_of(rb * CONV_ROWS, CONV_ROWS)
        conv = jnp.zeros((CONV_ROWS, CONV_WIDTH), F32) + db_ref[...]
        for back in range(CONV_KERNEL):
            a, b = divmod(back, SUBLANES)
            src = ext_u if b == 0 else shift_ref.at[b - 1]
            tap = CONV_KERNEL - 1 - back
            conv = conv + (src[pl.ds(r0 + HALO - SUBLANES * a, CONV_ROWS), :]
                           * jnp.concatenate([dw_ref[tap]] * (CONV_ROWS // SUBLANES), axis=0))
        conv_ref[pl.ds(r0, CONV_ROWS), :] = conv
        return carry

    lax.fori_loop(0, tm // CONV_ROWS, row_block, 0)
    conv = conv_ref[...]
    mu = jnp.mean(conv, axis=-1, keepdims=True)
    cen = conv - mu
    var = jnp.mean(cen * cen, axis=-1, keepdims=True)
    y = cen * lax.rsqrt(var + EPS) * cg_ref[...] + cb_ref[...]
    d_ref[...] = (y * _sigmoid(y)).astype(d_ref.dtype)

    ext_p[:HALO, :] = ext_p[tm:, :]
    ext_u[:HALO, :] = ext_u[tm:, :]


def _l1_mixer(x2, g, w, pw, ps, dw, db, cg, cb, batch, seq, tm):
    n = x2.shape[0]
    nt = seq // tm
    row = lambda b, t: (b * nt + t, 0)
    return pl.pallas_call(
        _l1_mixer_kernel,
        grid=(batch, nt),
        in_specs=[
            pl.BlockSpec((tm, D_MODEL), row),
            _const_spec((1, D_MODEL)),
            _const_spec((D_MODEL, ODD_IN_WIDTH)),
            _const_spec((len(POOL_WINDOWS), POOL_GROUP_DIM, POOL_GROUP_DIM)),
            _const_spec((1, POOL_WIDTH)),
            _const_spec((CONV_KERNEL, SUBLANES, CONV_WIDTH)),
            _const_spec((1, CONV_WIDTH)),
            _const_spec((1, CONV_WIDTH)),
            _const_spec((1, CONV_WIDTH)),
        ],
        out_specs=[pl.BlockSpec((tm, POOL_WIDTH), row), pl.BlockSpec((tm, CONV_WIDTH), row)],
        out_shape=[jax.ShapeDtypeStruct((n, POOL_WIDTH), BF16),
                   jax.ShapeDtypeStruct((n, CONV_WIDTH), BF16)],
        scratch_shapes=[pltpu.VMEM((HALO + tm, POOL_WIDTH), F32),
                        pltpu.VMEM((HALO + tm, CONV_WIDTH), F32),
                        pltpu.VMEM((SUBLANES - 1, HALO + tm, CONV_WIDTH), F32),
                        pltpu.VMEM((tm, CONV_WIDTH), F32)],
        compiler_params=_params("arbitrary", "arbitrary"),
        name="l1_mixer",
    )(x2, g, w, pw, ps, dw, db, cg, cb)


def _rope_tables(seq):
    half = HEAD_DIM // 2
    inv = jnp.power(ROPE_THETA, -jnp.arange(0, HEAD_DIM, 2, dtype=F32) / HEAD_DIM)
    ang = jnp.arange(seq, dtype=jnp.int32).astype(F32)[:, None] * inv[None, :]
    cos = jnp.cos(ang)
    sin = jnp.sin(ang)
    reps = LANES // HEAD_DIM
    return (jnp.tile(jnp.concatenate([cos, cos], axis=1), (1, reps)),
            jnp.tile(jnp.concatenate([-sin, sin], axis=1), (1, reps)))


def _tile(seq, want):
    return want if seq % want == 0 else seq


def kernel(x, mix_norm_0, w_in_0, lambda_q1_0, lambda_k1_0, lambda_q2_0, lambda_k2_0, subln_0, w_out_0, ffn_norm_0, w_gate_0, w_up_0, w_down_0, mix_norm_1, w_in_1, pool_w_1, pool_scale_1, dw_w_1, dw_b_1, conv_norm_g_1, conv_norm_b_1, w_out_1, ffn_norm_1, w_gate_1, w_up_1, w_down_1, final_norm):
    batch, seq, d = x.shape
    assert d == D_MODEL and seq % LANES == 0
    x2 = x.reshape(batch * seq, d)
    row = lambda v: v.reshape(1, -1).astype(F32)
    tm = _tile(seq, 512)
    tm_ffn = _tile(seq, 1024)

    cos_t, sin_t = _rope_tables(seq)
    proj, svt, dvt = _in_proj0(x2, row(mix_norm_0), w_in_0.astype(BF16), cos_t, sin_t,
                               batch, seq, tm)
    a_out = _sb_attention(proj, svt, batch, seq, _tile(seq, 256))
    lambda_init = 0.8 - 0.6 * math.exp(-0.3 * 0)
    b_out = _diff_attention(proj, dvt, row(lambda_q1_0), row(lambda_k1_0), row(lambda_q2_0),
                            row(lambda_k2_0), row(subln_0), batch, seq, _tile(seq, 512),
                            lambda_init)
    x2 = _out_ffn(x2, a_out, b_out, w_out_0.astype(BF16), row(ffn_norm_0),
                  w_gate_0.astype(BF16), w_up_0.astype(BF16), w_down_0.astype(BF16),
                  row(final_norm), tm_ffn, final_norm=False)

    c_out, d_out = _l1_mixer(x2, row(mix_norm_1), w_in_1.astype(BF16), pool_w_1.astype(BF16),
                             row(pool_scale_1),
                             jnp.broadcast_to(dw_w_1.astype(F32)[:, None, :],
                                              (CONV_KERNEL, SUBLANES, CONV_WIDTH)),
                             row(dw_b_1),
                             row(conv_norm_g_1), row(conv_norm_b_1), batch, seq, tm)
    out = _out_ffn(x2, c_out, d_out, w_out_1.astype(BF16), row(ffn_norm_1),
                   w_gate_1.astype(BF16), w_up_1.astype(BF16), w_down_1.astype(BF16),
                   row(final_norm), tm_ffn, final_norm=True)
    return out.reshape(batch, seq, d)
```

```python
import functools
import math

import jax
import jax.numpy as jnp
from jax import lax
from jax.experimental import pallas as pl
from jax.experimental.pallas import tpu as pltpu

F32 = jnp.float32
BF16 = jnp.bfloat16

D_MODEL = 1024
HEAD_DIM = 64
LANES = 128
SUBLANES = 8
CONV_ROWS = 64
SB_HEADS = 8
DIFF_HEADS = 4
SB_WIDTH = SB_HEADS * HEAD_DIM
DIFF_WIDTH = DIFF_HEADS * 2 * HEAD_DIM
EVEN_IN_WIDTH = 3 * SB_WIDTH + 3 * DIFF_WIDTH
CHUNK = 64
POOL_WINDOWS = (2, 4, 8, 16)
POOL_WIDTH = 512
POOL_GROUP_DIM = 128
CONV_WIDTH = 512
CONV_KERNEL = 31
ODD_IN_WIDTH = POOL_WIDTH + 2 * CONV_WIDTH
D_FF = 2816
FF_CHUNK = 256
ROPE_THETA = 10000.0
EPS = 1e-6
NEG = -1e30
SB_SCALE = HEAD_DIM ** -0.5
LOG2E = math.log2(math.e)
ONES_ROWS = 16
HALO = 32
QK_WIDTH = 2 * SB_WIDTH + 2 * DIFF_WIDTH
SB_LOG2_ZERO = -150.0 * math.log2(math.e)
F32_EXP2_MAX = 126.0
VMEM_LIMIT = 56 * 1024 * 1024


def _rms(x, g):
    return x * lax.rsqrt(jnp.mean(x * x, axis=-1, keepdims=True) + EPS) * g


def _sigmoid(x):
    return 1.0 / (1.0 + jnp.exp(-x))


def _dot(a, b):
    return jnp.dot(a, b, preferred_element_type=F32)


def _dot_nt(a, b):
    return lax.dot_general(a, b, (((1,), (1,)), ((), ())), preferred_element_type=F32)


def _params(*sem):
    return pltpu.CompilerParams(dimension_semantics=sem, vmem_limit_bytes=VMEM_LIMIT)


def _const_spec(shape, single_buffer=False):
    nd = len(shape)
    if single_buffer:
        return pl.BlockSpec(shape, lambda *_: (0,) * nd, pipeline_mode=pl.Buffered(1))
    return pl.BlockSpec(shape, lambda *_: (0,) * nd)


def _in_proj0_kernel(x_ref, g_ref, w_ref, cos_ref, sin_ref, o_ref, svt_ref, dvt_ref):
    h = _rms(x_ref[...], g_ref[...]).astype(BF16)
    tm = h.shape[0]
    lane = lax.broadcasted_iota(jnp.int32, (tm, LANES), 1)
    first_half = (lane % HEAD_DIM) < (HEAD_DIM // 2)
    cos = cos_ref[...]
    sin = sin_ref[...]
    seg = SB_WIDTH
    q_scale = SB_SCALE * LOG2E
    out_col = {0: 0, 1: 1, 3: 2, 4: 3}
    for c in range(EVEN_IN_WIDTH // seg):
        y = _dot(h, w_ref[:, c * seg:(c + 1) * seg])
        for s in range(seg // LANES):
            t = y[:, s * LANES:(s + 1) * LANES]
            if c in (2, 5):
                (svt_ref if c == 2 else dvt_ref)[0, s] = t.T.astype(BF16)
                continue
            if c in (3, 4):
                swapped = jnp.where(first_half,
                                    pltpu.roll(t, LANES - HEAD_DIM // 2, 1),
                                    pltpu.roll(t, HEAD_DIM // 2, 1))
                t = t * cos + swapped * sin
            if c in (0, 3):
                t = t * q_scale
            col = out_col[c] * seg + s * LANES
            o_ref[:, col:col + LANES] = t.astype(BF16)


def _in_proj0(x2, g, w, cos_t, sin_t, batch, seq, tm):
    n = x2.shape[0]
    per_seq = seq // tm
    vt_spec = pl.BlockSpec((1, SB_WIDTH // LANES, LANES, tm),
                           lambda i: (i // per_seq, 0, 0, i % per_seq))
    vt_shape = jax.ShapeDtypeStruct((batch, SB_WIDTH // LANES, LANES, seq), BF16)
    return pl.pallas_call(
        _in_proj0_kernel,
        grid=(n // tm,),
        in_specs=[
            pl.BlockSpec((tm, D_MODEL), lambda i: (i, 0)),
            _const_spec((1, D_MODEL)),
            _const_spec((D_MODEL, EVEN_IN_WIDTH)),
            pl.BlockSpec((tm, LANES), lambda i: (i % per_seq, 0)),
            pl.BlockSpec((tm, LANES), lambda i: (i % per_seq, 0)),
        ],
        out_specs=[pl.BlockSpec((tm, QK_WIDTH), lambda i: (i, 0)), vt_spec, vt_spec],
        out_shape=[jax.ShapeDtypeStruct((n, QK_WIDTH), BF16), vt_shape, vt_shape],
        compiler_params=_params("arbitrary"),
        name="l0_in_proj",
    )(x2, g, w, cos_t, sin_t)


def _log2_one_minus_beta(z):
    return -jnp.maximum(z, jnp.log2(1.0 + jnp.exp2(jnp.minimum(z, F32_EXP2_MAX))))


def _sb_kernel(q_ref, k_ref, vt_ref, o_ref, acc_ref, rest_ref, zl_ref, hi_ref, lo_ref, *, tq):
    lane = lax.broadcasted_iota(jnp.int32, (tq, LANES), 1)
    low = lane < HEAD_DIM
    tri_row = lax.broadcasted_iota(jnp.int32, (tq + ONES_ROWS, tq), 0)
    tri_col = lax.broadcasted_iota(jnp.int32, (tq + ONES_ROWS, tq), 1)
    tri = jnp.logical_or(tri_col > tri_row, tri_row >= tq).astype(BF16)
    head0_rows = lax.broadcasted_iota(jnp.int32, (LANES, tq), 0) < HEAD_DIM

    def sweep(q2, first, nblk, diagonal):
        nk = nblk * tq
        start = pl.multiple_of(first * tq, tq)
        z = _dot_nt(k_ref[pl.ds(start, nk), :], q2)
        log_1m = _log2_one_minus_beta(z)
        if diagonal:
            key = lax.broadcasted_iota(jnp.int32, (nk, 2 * tq), 0) - (nk - tq)
            qry = jnp.bitwise_and(lax.broadcasted_iota(jnp.int32, (nk, 2 * tq), 1), tq - 1)
            strict = key < qry
            log_1m = jnp.where(strict, log_1m, 0.0)
        hi = log_1m.astype(BF16)
        lo = (log_1m - hi.astype(F32)).astype(BF16)
        rest = rest_ref[...]
        log_w = []
        for b in reversed(range(nblk)):
            rows = slice(b * tq, (b + 1) * tq)
            after = _dot(tri, hi[rows]) + _dot(tri, lo[rows])
            log_w.append(z[rows] + log_1m[rows] + after[:tq] + rest)
            rest = rest + after[tq:tq + 1]
        w = jnp.exp2(jnp.concatenate(log_w[::-1], axis=0))
        if diagonal:
            w = jnp.where(strict, w, 0.0)
        acc_ref[...] += _dot(vt_ref[:, pl.ds(start, nk)], w.astype(BF16))
        rest_ref[...] = rest

    def masked_queries(i):
        q = q_ref[pl.ds(pl.multiple_of(i * tq, tq), tq), :]
        zero = jnp.zeros_like(q)
        return jnp.concatenate([jnp.where(low, q, zero), jnp.where(low, zero, q)], axis=0)

    def finish(i, next_block):
        def cond(st):
            j, worst = st
            return jnp.logical_and(j >= 0, worst > SB_LOG2_ZERO)

        def body(st):
            j, _ = st
            sweep(masked_queries(i), j, 1, False)
            return j - 1, jnp.max(rest_ref[...])

        lax.while_loop(cond, body, (next_block, jnp.max(rest_ref[...])))
        acc = acc_ref[...]
        o_ref[pl.ds(pl.multiple_of(i * tq, tq), tq), :] = jnp.where(
            head0_rows, acc[:, :tq], acc[:, tq:]).T.astype(o_ref.dtype)

    def stage_a(i, buf):
        start = pl.multiple_of((i - 1) * tq, tq)
        z = _dot_nt(k_ref[pl.ds(start, 2 * tq), :], masked_queries(i))
        log_1m = _log2_one_minus_beta(z)
        key = lax.broadcasted_iota(jnp.int32, (tq, 2 * tq), 0)
        qry = jnp.bitwise_and(lax.broadcasted_iota(jnp.int32, (tq, 2 * tq), 1), tq - 1)
        strict = key < qry
        for b in range(2):
            rows = slice(b * tq, (b + 1) * tq)
            lm = log_1m[rows]
            zl = z[rows] + lm
            if b == 1:
                lm = jnp.where(strict, lm, 0.0)
                zl = jnp.where(strict, zl, NEG)
            hi = lm.astype(BF16)
            hi_ref[buf, rows, :] = hi
            lo_ref[buf, rows, :] = (lm - hi.astype(F32)).astype(BF16)
            zl_ref[buf, rows, :] = zl

    def stage_b(i, buf):
        rest = jnp.zeros((1, 2 * tq), F32)
        log_w = []
        for b in (1, 0):
            rows = slice(b * tq, (b + 1) * tq)
            after = _dot(tri, hi_ref[buf, rows, :]) + _dot(tri, lo_ref[buf, rows, :])
            log_w.append(zl_ref[buf, rows, :] + after[:tq] + rest)
            rest = rest + after[tq:tq + 1]
        w = jnp.exp2(jnp.concatenate(log_w[::-1], axis=0)).astype(BF16)
        start = pl.multiple_of((i - 1) * tq, tq)
        acc_ref[...] = _dot(vt_ref[:, pl.ds(start, 2 * tq)], w)
        rest_ref[...] = rest
        finish(i, i - 2)

    acc_ref[...] = jnp.zeros_like(acc_ref)
    rest_ref[...] = jnp.zeros_like(rest_ref)
    sweep(masked_queries(0), 0, 1, True)
    finish(0, -1)

    nq = q_ref.shape[0] // tq
    if nq > 1:
        stage_a(1, 1)

        def body(t, carry):
            i = 2 * t + 1
            stage_a(i + 1, 0)
            stage_b(i, 1)
            stage_a(i + 2, 1)
            stage_b(i + 1, 0)
            return carry

        lax.fori_loop(0, nq // 2 - 1, body, 0)
        stage_b(nq - 1, 1)


def _sb_attention(proj, vt, batch, seq, tq):
    n = proj.shape[0]
    nq = seq // tq
    assert tq & (tq - 1) == 0 and (nq == 1 or nq % 2 == 0)
    k_off = SB_WIDTH // LANES
    return pl.pallas_call(
        functools.partial(_sb_kernel, tq=tq),
        grid=(batch, SB_WIDTH // LANES),
        in_specs=[
            pl.BlockSpec((seq, LANES), lambda b, p: (b, p)),
            pl.BlockSpec((seq, LANES), lambda b, p: (b, k_off + p)),
            pl.BlockSpec((None, None, LANES, seq), lambda b, p: (b, p, 0, 0)),
        ],
        out_specs=pl.BlockSpec((seq, LANES), lambda b, p: (b, p)),
        out_shape=jax.ShapeDtypeStruct((n, SB_WIDTH), BF16),
        scratch_shapes=[pltpu.VMEM((LANES, 2 * tq), F32), pltpu.VMEM((1, 2 * tq), F32),
                        pltpu.VMEM((2, 2 * tq, 2 * tq), F32),
                        pltpu.VMEM((2, 2 * tq, 2 * tq), BF16),
                        pltpu.VMEM((2, 2 * tq, 2 * tq), BF16)],
        compiler_params=_params("arbitrary", "arbitrary"),
        name="sb_attention",
    )(proj, proj, vt)


def _diff_kernel(lq1_ref, lk1_ref, lq2_ref, lk2_ref, g_ref, q_ref, k_ref, vt_ref, o_ref,
                 m_ref, acc_ref, s_ref, bmax_ref, *, tq, lambda_init):
    lane = lax.broadcasted_iota(jnp.int32, (tq, LANES), 1)
    low = lane < HEAD_DIM
    ones = jnp.ones((ONES_ROWS, tq), BF16)
    lam = (jnp.exp(jnp.sum(lq1_ref[...] * lk1_ref[...], axis=1, keepdims=True))
           - jnp.exp(jnp.sum(lq2_ref[...] * lk2_ref[...], axis=1, keepdims=True))
           + lambda_init)

    def query_block(i, carry):
        rows = pl.ds(pl.multiple_of(i * tq, tq), tq)
        q = q_ref[rows, :]
        zero = jnp.zeros_like(q)
        q_maps = (jnp.where(low, q, zero), jnp.where(low, zero, q))
        m_ref[...] = jnp.full_like(m_ref, NEG)
        acc_ref[...] = jnp.zeros_like(acc_ref)

        def scores(j, buf, diagonal=False):
            k = k_ref[pl.ds(pl.multiple_of(j * tq, tq), tq), :]
            for mp in range(2):
                s = _dot_nt(k, q_maps[mp])
                if diagonal:
                    visible = ((lax.broadcasted_iota(jnp.int32, (tq, tq), 0) // CHUNK)
                               <= (lax.broadcasted_iota(jnp.int32, (tq, tq), 1) // CHUNK))
                    s = jnp.where(visible, s, NEG)
                s_ref[buf, mp] = s
                bmax_ref[buf, mp] = jnp.max(s, axis=0, keepdims=True)

        def accumulate(j, buf):
            vt = jnp.concatenate([vt_ref[:, pl.ds(pl.multiple_of(j * tq, tq), tq)], ones],
                                 axis=0)
            for mp in range(2):
                m_old = m_ref[mp]
                m_new = jnp.maximum(m_old, bmax_ref[buf, mp])
                p = jnp.exp2(s_ref[buf, mp] - m_new).astype(BF16)
                acc_ref[mp] = jnp.exp2(m_old - m_new) * acc_ref[mp] + _dot(vt, p)
                m_ref[mp] = m_new

        scores(i, 0, diagonal=True)
        pairs = i // 2

        def pair(jj):
            scores(2 * jj, 1)
            accumulate(jnp.where(jj == 0, i, 2 * jj - 1), 0)
            scores(2 * jj + 1, 0)
            accumulate(2 * jj, 1)

        def two_pairs(t, c):
            pair(2 * t)
            pair(2 * t + 1)
            return c

        def one_pair(jj, c):
            pair(jj)
            return c

        lax.fori_loop(0, pairs // 2, two_pairs, 0)
        lax.fori_loop(2 * (pairs // 2), pairs, one_pair, 0)
        held = jnp.where(pairs == 0, i, 2 * pairs - 1)

        @pl.when(i % 2 == 1)
        def _():
            scores(i - 1, 1)
            accumulate(held, 0)
            accumulate(i - 1, 1)

        @pl.when(i % 2 == 0)
        def _():
            accumulate(held, 0)

        acc1 = acc_ref[0]
        acc2 = acc_ref[1]
        out_t = (acc1[:LANES] * (1.0 / acc1[LANES:LANES + 1])
                 - lam * (acc2[:LANES] * (1.0 / acc2[LANES:LANES + 1])))
        out = _rms(out_t.T, g_ref[...]) * (1.0 - lambda_init)
        o_ref[rows, :] = out.astype(o_ref.dtype)
        return carry

    lax.fori_loop(0, q_ref.shape[0] // tq, query_block, 0)


def _diff_attention(proj, vt, lq1, lk1, lq2, lk2, subln_g, batch, seq, tq, lambda_init):
    n = proj.shape[0]
    q_off = 2 * SB_WIDTH // LANES
    k_off = q_off + DIFF_WIDTH // LANES
    vec = _const_spec((1, HEAD_DIM))
    return pl.pallas_call(
        functools.partial(_diff_kernel, tq=tq, lambda_init=lambda_init),
        grid=(batch, DIFF_HEADS),
        in_specs=[
            vec, vec, vec, vec,
            _const_spec((1, LANES)),
            pl.BlockSpec((seq, LANES), lambda b, h: (b, q_off + h)),
            pl.BlockSpec((seq, LANES), lambda b, h: (b, k_off + h)),
            pl.BlockSpec((None, None, LANES, seq), lambda b, h: (b, h, 0, 0)),
        ],
        out_specs=pl.BlockSpec((seq, LANES), lambda b, h: (b, h)),
        out_shape=jax.ShapeDtypeStruct((n, DIFF_WIDTH), BF16),
        scratch_shapes=[pltpu.VMEM((2, 1, tq), F32),
                        pltpu.VMEM((2, LANES + ONES_ROWS, tq), F32),
                        pltpu.VMEM((2, 2, tq, tq), F32),
                        pltpu.VMEM((2, 2, 1, tq), F32)],
        compiler_params=_params("arbitrary", "arbitrary"),
        name="diff_attention",
    )(lq1, lk1, lq2, lk2, subln_g, proj, proj, vt)


def _out_ffn_kernel(x_ref, a_ref, b_ref, wo_ref, gf_ref, wg_ref, wu_ref, wd_ref, gn_ref, o_ref,
                    acc_ref, *, final_norm):
    half = a_ref.shape[1]
    x1 = x_ref[...] + _dot(a_ref[...], wo_ref[:half, :]) + _dot(b_ref[...], wo_ref[half:, :])
    h = _rms(x1, gf_ref[...]).astype(BF16)
    acc_ref[...] = x1

    def body(c, carry):
        cols = pl.ds(pl.multiple_of(c * FF_CHUNK, FF_CHUNK), FF_CHUNK)
        gate = _dot(h, wg_ref[:, cols])
        act = (gate * _sigmoid(gate) * _dot(h, wu_ref[:, cols])).astype(BF16)
        acc_ref[...] += _dot(act, wd_ref[cols, :])
        return carry

    lax.fori_loop(0, D_FF // FF_CHUNK, body, 0)
    y = acc_ref[...]
    if final_norm:
        y = _rms(y, gn_ref[...])
    o_ref[...] = y


def _out_ffn(x2, a, b, wo, gf, wg, wu, wd, gn, tm, final_norm):
    n = x2.shape[0]
    half = a.shape[1]
    return pl.pallas_call(
        functools.partial(_out_ffn_kernel, final_norm=final_norm),
        grid=(n // tm,),
        in_specs=[
            pl.BlockSpec((tm, D_MODEL), lambda i: (i, 0)),
            pl.BlockSpec((tm, half), lambda i: (i, 0)),
            pl.BlockSpec((tm, half), lambda i: (i, 0)),
            _const_spec((2 * half, D_MODEL), single_buffer=True),
            _const_spec((1, D_MODEL)),
            _const_spec((D_MODEL, D_FF), single_buffer=True),
            _const_spec((D_MODEL, D_FF), single_buffer=True),
            _const_spec((D_FF, D_MODEL), single_buffer=True),
            _const_spec((1, D_MODEL)),
        ],
        out_specs=pl.BlockSpec((tm, D_MODEL), lambda i: (i, 0)),
        out_shape=jax.ShapeDtypeStruct((n, D_MODEL), F32),
        scratch_shapes=[pltpu.VMEM((tm, D_MODEL), F32)],
        compiler_params=_params("arbitrary"),
        name="out_proj_ffn",
    )(x2, a, b, wo, gf, wg, wu, wd, gn)


def _l1_mixer_kernel(x_ref, g_ref, w_ref, pw_ref, ps_ref, dw_ref, db_ref, cg_ref, cb_ref,
                     c_ref, d_ref, ext_p, ext_u, shift_ref, conv_ref):
    ti = pl.program_id(1)
    tm = x_ref.shape[0]

    @pl.when(ti == 0)
    def _():
        ext_p[:HALO, :] = jnp.zeros((HALO, POOL_WIDTH), F32)
        ext_u[:HALO, :] = jnp.zeros((HALO, CONV_WIDTH), F32)

    h = _rms(x_ref[...], g_ref[...]).astype(BF16)
    xp = _dot(h, w_ref[:, :POOL_WIDTH])
    xa = _dot(h, w_ref[:, POOL_WIDTH:POOL_WIDTH + CONV_WIDTH])
    xg = _dot(h, w_ref[:, POOL_WIDTH + CONV_WIDTH:])
    ext_p[HALO:, :] = xp
    ext_u[HALO:, :] = xa * _sigmoid(xg)

    pos1 = (ti * tm + 1 + lax.broadcasted_iota(jnp.int32, (tm, 1), 0)).astype(F32)
    for g, win in enumerate(POOL_WINDOWS):
        cols = slice(g * POOL_GROUP_DIM, (g + 1) * POOL_GROUP_DIM)
        total = ext_p[HALO:, cols]
        for back in range(1, win):
            total = total + ext_p[HALO - back:HALO - back + tm, cols]
        pooled = total / jnp.minimum(pos1, float(win)) - ext_p[HALO:, cols]
        mixed = _dot(pooled.astype(BF16), pw_ref[g])
        c_ref[:, cols] = (mixed * ps_ref[:, cols]).astype(c_ref.dtype)

    for b in range(1, SUBLANES):
        shift_ref[b - 1, SUBLANES:, :] = ext_u[SUBLANES - b:HALO + tm - b, :]

    def row_block(rb, carry):
        r0 = pl.multiple_of(rb * CONV_ROWS, CONV_ROWS)
        conv = jnp.zeros((CONV_ROWS, CONV_WIDTH), F32) + db_ref[...]
        for back in range(CONV_KERNEL):
            a, b = divmod(back, SUBLANES)
            src = ext_u if b == 0 else shift_ref.at[b - 1]
            tap = CONV_KERNEL - 1 - back
            conv = conv + (src[pl.ds(r0 + HALO - SUBLANES * a, CONV_ROWS), :]
                           * jnp.concatenate([dw_ref[tap]] * (CONV_ROWS // SUBLANES), axis=0))
        conv_ref[pl.ds(r0, CONV_ROWS), :] = conv
        return carry

    lax.fori_loop(0, tm // CONV_ROWS, row_block, 0)
    conv = conv_ref[...]
    mu = jnp.mean(conv, axis=-1, keepdims=True)
    cen = conv - mu
    var = jnp.mean(cen * cen, axis=-1, keepdims=True)
    y = cen * lax.rsqrt(var + EPS) * cg_ref[...] + cb_ref[...]
    d_ref[...] = (y * _sigmoid(y)).astype(d_ref.dtype)

    ext_p[:HALO, :] = ext_p[tm:, :]
    ext_u[:HALO, :] = ext_u[tm:, :]


def _l1_mixer(x2, g, w, pw, ps, dw, db, cg, cb, batch, seq, tm):
    n = x2.shape[0]
    nt = seq // tm
    row = lambda b, t: (b * nt + t, 0)
    return pl.pallas_call(
        _l1_mixer_kernel,
        grid=(batch, nt),
        in_specs=[
            pl.BlockSpec((tm, D_MODEL), row),
            _const_spec((1, D_MODEL)),
            _const_spec((D_MODEL, ODD_IN_WIDTH)),
            _const_spec((len(POOL_WINDOWS), POOL_GROUP_DIM, POOL_GROUP_DIM)),
            _const_spec((1, POOL_WIDTH)),
            _const_spec((CONV_KERNEL, SUBLANES, CONV_WIDTH)),
            _const_spec((1, CONV_WIDTH)),
            _const_spec((1, CONV_WIDTH)),
            _const_spec((1, CONV_WIDTH)),
        ],
        out_specs=[pl.BlockSpec((tm, POOL_WIDTH), row), pl.BlockSpec((tm, CONV_WIDTH), row)],
        out_shape=[jax.ShapeDtypeStruct((n, POOL_WIDTH), BF16),
                   jax.ShapeDtypeStruct((n, CONV_WIDTH), BF16)],
        scratch_shapes=[pltpu.VMEM((HALO + tm, POOL_WIDTH), F32),
                        pltpu.VMEM((HALO + tm, CONV_WIDTH), F32),
                        pltpu.VMEM((SUBLANES - 1, HALO + tm, CONV_WIDTH), F32),
                        pltpu.VMEM((tm, CONV_WIDTH), F32)],
        compiler_params=_params("arbitrary", "arbitrary"),
        name="l1_mixer",
    )(x2, g, w, pw, ps, dw, db, cg, cb)


def _rope_tables(seq):
    half = HEAD_DIM // 2
    inv = jnp.power(ROPE_THETA, -jnp.arange(0, HEAD_DIM, 2, dtype=F32) / HEAD_DIM)
    ang = jnp.arange(seq, dtype=jnp.int32).astype(F32)[:, None] * inv[None, :]
    cos = jnp.cos(ang)
    sin = jnp.sin(ang)
    reps = LANES // HEAD_DIM
    return (jnp.tile(jnp.concatenate([cos, cos], axis=1), (1, reps)),
            jnp.tile(jnp.concatenate([-sin, sin], axis=1), (1, reps)))


def _tile(seq, want):
    return want if seq % want == 0 else seq


def kernel(x, mix_norm_0, w_in_0, lambda_q1_0, lambda_k1_0, lambda_q2_0, lambda_k2_0, subln_0, w_out_0, ffn_norm_0, w_gate_0, w_up_0, w_down_0, mix_norm_1, w_in_1, pool_w_1, pool_scale_1, dw_w_1, dw_b_1, conv_norm_g_1, conv_norm_b_1, w_out_1, ffn_norm_1, w_gate_1, w_up_1, w_down_1, final_norm):
    batch, seq, d = x.shape
    assert d == D_MODEL and seq % LANES == 0
    x2 = x.reshape(batch * seq, d)
    row = lambda v: v.reshape(1, -1).astype(F32)
    tm = _tile(seq, 512)
    tm_ffn = _tile(seq, 1024)

    cos_t, sin_t = _rope_tables(seq)
    proj, svt, dvt = _in_proj0(x2, row(mix_norm_0), w_in_0.astype(BF16), cos_t, sin_t,
                               batch, seq, tm)
    a_out = _sb_attention(proj, svt, batch, seq, _tile(seq, 256))
    lambda_init = 0.8 - 0.6 * math.exp(-0.3 * 0)
    b_out = _diff_attention(proj, dvt, row(lambda_q1_0), row(lambda_k1_0), row(lambda_q2_0),
                            row(lambda_k2_0), row(subln_0), batch, seq, _tile(seq, 512),
                            lambda_init)
    x2 = _out_ffn(x2, a_out, b_out, w_out_0.astype(BF16), row(ffn_norm_0),
                  w_gate_0.astype(BF16), w_up_0.astype(BF16), w_down_0.astype(BF16),
                  row(final_norm), tm_ffn, final_norm=False)

    c_out, d_out = _l1_mixer(x2, row(mix_norm_1), w_in_1.astype(BF16), pool_w_1.astype(BF16),
                             row(pool_scale_1),
                             jnp.broadcast_to(dw_w_1.astype(F32)[:, None, :],
                                              (CONV_KERNEL, SUBLANES, CONV_WIDTH)),
                             row(dw_b_1),
                             row(conv_norm_g_1), row(conv_norm_b_1), batch, seq, tm)
    out = _out_ffn(x2, c_out, d_out, w_out_1.astype(BF16), row(ffn_norm_1),
                   w_gate_1.astype(BF16), w_up_1.astype(BF16), w_down_1.astype(BF16),
                   row(final_norm), tm_ffn, final_norm=True)
    return out.reshape(batch, seq, d)
```

```python
import functools
import math

import jax
import jax.numpy as jnp
from jax import lax
from jax.experimental import pallas as pl
from jax.experimental.pallas import tpu as pltpu

F32 = jnp.float32
BF16 = jnp.bfloat16

D_MODEL = 1024
HEAD_DIM = 64
LANES = 128
SUBLANES = 8
CONV_ROWS = 64
SB_HEADS = 8
DIFF_HEADS = 4
SB_WIDTH = SB_HEADS * HEAD_DIM
DIFF_WIDTH = DIFF_HEADS * 2 * HEAD_DIM
EVEN_IN_WIDTH = 3 * SB_WIDTH + 3 * DIFF_WIDTH
CHUNK = 64
POOL_WINDOWS = (2, 4, 8, 16)
POOL_WIDTH = 512
POOL_GROUP_DIM = 128
CONV_WIDTH = 512
CONV_KERNEL = 31
ODD_IN_WIDTH = POOL_WIDTH + 2 * CONV_WIDTH
D_FF = 2816
FF_CHUNK = 256
ROPE_THETA = 10000.0
EPS = 1e-6
NEG = -1e30
SB_SCALE = HEAD_DIM ** -0.5
LOG2E = math.log2(math.e)
ONES_ROWS = 16
HALO = 32
QK_WIDTH = 2 * SB_WIDTH + 2 * DIFF_WIDTH
SB_LOG2_ZERO = -150.0 * math.log2(math.e)
F32_EXP2_MAX = 126.0
VMEM_LIMIT = 56 * 1024 * 1024


def _rms(x, g):
    return x * lax.rsqrt(jnp.mean(x * x, axis=-1, keepdims=True) + EPS) * g


def _sigmoid(x):
    return 1.0 / (1.0 + jnp.exp(-x))


def _dot(a, b):
    return jnp.dot(a, b, preferred_element_type=F32)


def _dot_nt(a, b):
    return lax.dot_general(a, b, (((1,), (1,)), ((), ())), preferred_element_type=F32)


def _params(*sem):
    return pltpu.CompilerParams(dimension_semantics=sem, vmem_limit_bytes=VMEM_LIMIT)


def _const_spec(shape, single_buffer=False):
    nd = len(shape)
    if single_buffer:
        return pl.BlockSpec(shape, lambda *_: (0,) * nd, pipeline_mode=pl.Buffered(1))
    return pl.BlockSpec(shape, lambda *_: (0,) * nd)


def _in_proj0_kernel(x_ref, g_ref, w_ref, cos_ref, sin_ref, o_ref, svt_ref, dvt_ref):
    h = _rms(x_ref[...], g_ref[...]).astype(BF16)
    tm = h.shape[0]
    lane = lax.broadcasted_iota(jnp.int32, (tm, LANES), 1)
    first_half = (lane % HEAD_DIM) < (HEAD_DIM // 2)
    cos = cos_ref[...]
    sin = sin_ref[...]
    seg = SB_WIDTH
    q_scale = SB_SCALE * LOG2E
    out_col = {0: 0, 1: 1, 3: 2, 4: 3}
    for c in range(EVEN_IN_WIDTH // seg):
        y = _dot(h, w_ref[:, c * seg:(c + 1) * seg])
        for s in range(seg // LANES):
            t = y[:, s * LANES:(s + 1) * LANES]
            if c in (2, 5):
                (svt_ref if c == 2 else dvt_ref)[0, s] = t.T.astype(BF16)
                continue
            if c in (3, 4):
                swapped = jnp.where(first_half,
                                    pltpu.roll(t, LANES - HEAD_DIM // 2, 1),
                                    pltpu.roll(t, HEAD_DIM // 2, 1))
                t = t * cos + swapped * sin
            if c in (0, 3):
                t = t * q_scale
            col = out_col[c] * seg + s * LANES
            o_ref[:, col:col + LANES] = t.astype(BF16)


def _in_proj0(x2, g, w, cos_t, sin_t, batch, seq, tm):
    n = x2.shape[0]
    per_seq = seq // tm
    vt_spec = pl.BlockSpec((1, SB_WIDTH // LANES, LANES, tm),
                           lambda i: (i // per_seq, 0, 0, i % per_seq))
    vt_shape = jax.ShapeDtypeStruct((batch, SB_WIDTH // LANES, LANES, seq), BF16)
    return pl.pallas_call(
        _in_proj0_kernel,
        grid=(n // tm,),
        in_specs=[
            pl.BlockSpec((tm, D_MODEL), lambda i: (i, 0)),
            _const_spec((1, D_MODEL)),
            _const_spec((D_MODEL, EVEN_IN_WIDTH)),
            pl.BlockSpec((tm, LANES), lambda i: (i % per_seq, 0)),
            pl.BlockSpec((tm, LANES), lambda i: (i % per_seq, 0)),
        ],
        out_specs=[pl.BlockSpec((tm, QK_WIDTH), lambda i: (i, 0)), vt_spec, vt_spec],
        out_shape=[jax.ShapeDtypeStruct((n, QK_WIDTH), BF16), vt_shape, vt_shape],
        compiler_params=_params("arbitrary"),
        name="l0_in_proj",
    )(x2, g, w, cos_t, sin_t)


def _log2_one_minus_beta(z):
    return -jnp.maximum(z, jnp.log2(1.0 + jnp.exp2(jnp.minimum(z, F32_EXP2_MAX))))


def _sb_kernel(q_ref, k_ref, vt_ref, o_ref, acc_ref, rest_ref, zl_ref, hi_ref, lo_ref, *, tq):
    lane = lax.broadcasted_iota(jnp.int32, (tq, LANES), 1)
    low = lane < HEAD_DIM
    tri_row = lax.broadcasted_iota(jnp.int32, (tq + ONES_ROWS, tq), 0)
    tri_col = lax.broadcasted_iota(jnp.int32, (tq + ONES_ROWS, tq), 1)
    tri = jnp.logical_or(tri_col > tri_row, tri_row >= tq).astype(BF16)
    head0_rows = lax.broadcasted_iota(jnp.int32, (LANES, tq), 0) < HEAD_DIM

    def sweep(q2, first, nblk, diagonal):
        nk = nblk * tq
        start = pl.multiple_of(first * tq, tq)
        z = _dot_nt(k_ref[pl.ds(start, nk), :], q2)
        log_1m = _log2_one_minus_beta(z)
        if diagonal:
            key = lax.broadcasted_iota(jnp.int32, (nk, 2 * tq), 0) - (nk - tq)
            qry = jnp.bitwise_and(lax.broadcasted_iota(jnp.int32, (nk, 2 * tq), 1), tq - 1)
            strict = key < qry
            log_1m = jnp.where(strict, log_1m, 0.0)
        hi = log_1m.astype(BF16)
        lo = (log_1m - hi.astype(F32)).astype(BF16)
        rest = rest_ref[...]
        log_w = []
        for b in reversed(range(nblk)):
            rows = slice(b * tq, (b + 1) * tq)
            after = _dot(tri, hi[rows]) + _dot(tri, lo[rows])
            log_w.append(z[rows] + log_1m[rows] + after[:tq] + rest)
            rest = rest + after[tq:tq + 1]
        w = jnp.exp2(jnp.concatenate(log_w[::-1], axis=0))
        if diagonal:
            w = jnp.where(strict, w, 0.0)
        acc_ref[...] += _dot(vt_ref[:, pl.ds(start, nk)], w.astype(BF16))
        rest_ref[...] = rest

    def masked_queries(i):
        q = q_ref[pl.ds(pl.multiple_of(i * tq, tq), tq), :]
        zero = jnp.zeros_like(q)
        return jnp.concatenate([jnp.where(low, q, zero), jnp.where(low, zero, q)], axis=0)

    def finish(i, next_block):
        def cond(st):
            j, worst = st
            return jnp.logical_and(j >= 0, worst > SB_LOG2_ZERO)

        def body(st):
            j, _ = st
            sweep(masked_queries(i), j, 1, False)
            return j - 1, jnp.max(rest_ref[...])

        lax.while_loop(cond, body, (next_block, jnp.max(rest_ref[...])))
        acc = acc_ref[...]
        o_ref[pl.ds(pl.multiple_of(i * tq, tq), tq), :] = jnp.where(
            head0_rows, acc[:, :tq], acc[:, tq:]).T.astype(o_ref.dtype)

    def stage_a(i, buf):
        start = pl.multiple_of((i - 1) * tq, tq)
        z = _dot_nt(k_ref[pl.ds(start, 2 * tq), :], masked_queries(i))
        log_1m = _log2_one_minus_beta(z)
        key = lax.broadcasted_iota(jnp.int32, (tq, 2 * tq), 0)
        qry = jnp.bitwise_and(lax.broadcasted_iota(jnp.int32, (tq, 2 * tq), 1), tq - 1)
        strict = key < qry
        for b in range(2):
            rows = slice(b * tq, (b + 1) * tq)
            lm = log_1m[rows]
            zl = z[rows] + lm
            if b == 1:
                lm = jnp.where(strict, lm, 0.0)
                zl = jnp.where(strict, zl, NEG)
            hi = lm.astype(BF16)
            hi_ref[buf, rows, :] = hi
            lo_ref[buf, rows, :] = (lm - hi.astype(F32)).astype(BF16)
            zl_ref[buf, rows, :] = zl

    def stage_b(i, buf):
        rest = jnp.zeros((1, 2 * tq), F32)
        log_w = []
        for b in (1, 0):
            rows = slice(b * tq, (b + 1) * tq)
            after = _dot(tri, hi_ref[buf, rows, :]) + _dot(tri, lo_ref[buf, rows, :])
            log_w.append(zl_ref[buf, rows, :] + after[:tq] + rest)
            rest = rest + after[tq:tq + 1]
        w = jnp.exp2(jnp.concatenate(log_w[::-1], axis=0)).astype(BF16)
        start = pl.multiple_of((i - 1) * tq, tq)
        acc_ref[...] = _dot(vt_ref[:, pl.ds(start, 2 * tq)], w)
        rest_ref[...] = rest
        finish(i, i - 2)

    acc_ref[...] = jnp.zeros_like(acc_ref)
    rest_ref[...] = jnp.zeros_like(rest_ref)
    sweep(masked_queries(0), 0, 1, True)
    finish(0, -1)

    nq = q_ref.shape[0] // tq
    if nq > 1:
        stage_a(1, 1)

        def body(t, carry):
            i = 2 * t + 1
            stage_a(i + 1, 0)
            stage_b(i, 1)
            stage_a(i + 2, 1)
            stage_b(i + 1, 0)
            return carry

        lax.fori_loop(0, nq // 2 - 1, body, 0)
        stage_b(nq - 1, 1)


def _sb_attention(proj, vt, batch, seq, tq):
    n = proj.shape[0]
    nq = seq // tq
    assert tq & (tq - 1) == 0 and (nq == 1 or nq % 2 == 0)
    k_off = SB_WIDTH // LANES
    return pl.pallas_call(
        functools.partial(_sb_kernel, tq=tq),
        grid=(batch, SB_WIDTH // LANES),
        in_specs=[
            pl.BlockSpec((seq, LANES), lambda b, p: (b, p)),
            pl.BlockSpec((seq, LANES), lambda b, p: (b, k_off + p)),
            pl.BlockSpec((None, None, LANES, seq), lambda b, p: (b, p, 0, 0)),
        ],
        out_specs=pl.BlockSpec((seq, LANES), lambda b, p: (b, p)),
        out_shape=jax.ShapeDtypeStruct((n, SB_WIDTH), BF16),
        scratch_shapes=[pltpu.VMEM((LANES, 2 * tq), F32), pltpu.VMEM((1, 2 * tq), F32),
                        pltpu.VMEM((2, 2 * tq, 2 * tq), F32),
                        pltpu.VMEM((2, 2 * tq, 2 * tq), BF16),
                        pltpu.VMEM((2, 2 * tq, 2 * tq), BF16)],
        compiler_params=_params("arbitrary", "arbitrary"),
        name="sb_attention",
    )(proj, proj, vt)


def _diff_kernel(lq1_ref, lk1_ref, lq2_ref, lk2_ref, g_ref, q_ref, k_ref, vt_ref, o_ref,
                 m_ref, acc_ref, s_ref, bmax_ref, *, tq, lambda_init):
    lane = lax.broadcasted_iota(jnp.int32, (tq, LANES), 1)
    low = lane < HEAD_DIM
    ones = jnp.ones((ONES_ROWS, tq), BF16)
    lam = (jnp.exp(jnp.sum(lq1_ref[...] * lk1_ref[...], axis=1, keepdims=True))
           - jnp.exp(jnp.sum(lq2_ref[...] * lk2_ref[...], axis=1, keepdims=True))
           + lambda_init)

    def query_block(i, carry):
        rows = pl.ds(pl.multiple_of(i * tq, tq), tq)
        q = q_ref[rows, :]
        zero = jnp.zeros_like(q)
        q_maps = (jnp.where(low, q, zero), jnp.where(low, zero, q))
        m_ref[...] = jnp.full_like(m_ref, NEG)
        acc_ref[...] = jnp.zeros_like(acc_ref)

        def scores(j, buf, diagonal=False):
            k = k_ref[pl.ds(pl.multiple_of(j * tq, tq), tq), :]
            for mp in range(2):
                s = _dot_nt(k, q_maps[mp])
                if diagonal:
                    visible = ((lax.broadcasted_iota(jnp.int32, (tq, tq), 0) // CHUNK)
                               <= (lax.broadcasted_iota(jnp.int32, (tq, tq), 1) // CHUNK))
                    s = jnp.where(visible, s, NEG)
                s_ref[buf, mp] = s
                bmax_ref[buf, mp] = jnp.max(s, axis=0, keepdims=True)

        def accumulate(j, buf):
            vt = jnp.concatenate([vt_ref[:, pl.ds(pl.multiple_of(j * tq, tq), tq)], ones],
                                 axis=0)
            for mp in range(2):
                m_old = m_ref[mp]
                m_new = jnp.maximum(m_old, bmax_ref[buf, mp])
                p = jnp.exp2(s_ref[buf, mp] - m_new).astype(BF16)
                acc_ref[mp] = jnp.exp2(m_old - m_new) * acc_ref[mp] + _dot(vt, p)
                m_ref[mp] = m_new

        scores(i, 0, diagonal=True)
        pairs = i // 2

        def pair(jj):
            scores(2 * jj, 1)
            accumulate(jnp.where(jj == 0, i, 2 * jj - 1), 0)
            scores(2 * jj + 1, 0)
            accumulate(2 * jj, 1)

        def pairs_per_trip(n):
            def trip(t, c):
                for u in range(n):
                    pair(n * t + u)
                return c
            return trip

        lax.fori_loop(0, pairs // 4, pairs_per_trip(4), 0)
        lax.fori_loop(2 * (pairs // 4), pairs // 2, pairs_per_trip(2), 0)
        lax.fori_loop(2 * (pairs // 2), pairs, pairs_per_trip(1), 0)
        held = jnp.where(pairs == 0, i, 2 * pairs - 1)

        @pl.when(i % 2 == 1)
        def _():
            scores(i - 1, 1)
            accumulate(held, 0)
            accumulate(i - 1, 1)

        @pl.when(i % 2 == 0)
        def _():
            accumulate(held, 0)

        acc1 = acc_ref[0]
        acc2 = acc_ref[1]
        out_t = (acc1[:LANES] * (1.0 / acc1[LANES:LANES + 1])
                 - lam * (acc2[:LANES] * (1.0 / acc2[LANES:LANES + 1])))
        out = _rms(out_t.T, g_ref[...]) * (1.0 - lambda_init)
        o_ref[rows, :] = out.astype(o_ref.dtype)
        return carry

    lax.fori_loop(0, q_ref.shape[0] // tq, query_block, 0)


def _diff_attention(proj, vt, lq1, lk1, lq2, lk2, subln_g, batch, seq, tq, lambda_init):
    n = proj.shape[0]
    q_off = 2 * SB_WIDTH // LANES
    k_off = q_off + DIFF_WIDTH // LANES
    vec = _const_spec((1, HEAD_DIM))
    return pl.pallas_call(
        functools.partial(_diff_kernel, tq=tq, lambda_init=lambda_init),
        grid=(batch, DIFF_HEADS),
        in_specs=[
            vec, vec, vec, vec,
            _const_spec((1, LANES)),
            pl.BlockSpec((seq, LANES), lambda b, h: (b, q_off + h)),
            pl.BlockSpec((seq, LANES), lambda b, h: (b, k_off + h)),
            pl.BlockSpec((None, None, LANES, seq), lambda b, h: (b, h, 0, 0)),
        ],
        out_specs=pl.BlockSpec((seq, LANES), lambda b, h: (b, h)),
        out_shape=jax.ShapeDtypeStruct((n, DIFF_WIDTH), BF16),
        scratch_shapes=[pltpu.VMEM((2, 1, tq), F32),
                        pltpu.VMEM((2, LANES + ONES_ROWS, tq), F32),
                        pltpu.VMEM((2, 2, tq, tq), F32),
                        pltpu.VMEM((2, 2, 1, tq), F32)],
        compiler_params=_params("arbitrary", "arbitrary"),
        name="diff_attention",
    )(lq1, lk1, lq2, lk2, subln_g, proj, proj, vt)


def _out_ffn_kernel(x_ref, a_ref, b_ref, wo_ref, gf_ref, wg_ref, wu_ref, wd_ref, gn_ref, o_ref,
                    acc_ref, *, final_norm):
    half = a_ref.shape[1]
    x1 = x_ref[...] + _dot(a_ref[...], wo_ref[:half, :]) + _dot(b_ref[...], wo_ref[half:, :])
    h = _rms(x1, gf_ref[...]).astype(BF16)
    acc_ref[...] = x1

    def body(c, carry):
        cols = pl.ds(pl.multiple_of(c * FF_CHUNK, FF_CHUNK), FF_CHUNK)
        gate = _dot(h, wg_ref[:, cols])
        act = (gate * _sigmoid(gate) * _dot(h, wu_ref[:, cols])).astype(BF16)
        acc_ref[...] += _dot(act, wd_ref[cols, :])
        return carry

    lax.fori_loop(0, D_FF // FF_CHUNK, body, 0)
    y = acc_ref[...]
    if final_norm:
        y = _rms(y, gn_ref[...])
    o_ref[...] = y


def _out_ffn(x2, a, b, wo, gf, wg, wu, wd, gn, tm, final_norm):
    n = x2.shape[0]
    half = a.shape[1]
    return pl.pallas_call(
        functools.partial(_out_ffn_kernel, final_norm=final_norm),
        grid=(n // tm,),
        in_specs=[
            pl.BlockSpec((tm, D_MODEL), lambda i: (i, 0)),
            pl.BlockSpec((tm, half), lambda i: (i, 0)),
            pl.BlockSpec((tm, half), lambda i: (i, 0)),
            _const_spec((2 * half, D_MODEL), single_buffer=True),
            _const_spec((1, D_MODEL)),
            _const_spec((D_MODEL, D_FF), single_buffer=True),
            _const_spec((D_MODEL, D_FF), single_buffer=True),
            _const_spec((D_FF, D_MODEL), single_buffer=True),
            _const_spec((1, D_MODEL)),
        ],
        out_specs=pl.BlockSpec((tm, D_MODEL), lambda i: (i, 0)),
        out_shape=jax.ShapeDtypeStruct((n, D_MODEL), F32),
        scratch_shapes=[pltpu.VMEM((tm, D_MODEL), F32)],
        compiler_params=_params("arbitrary"),
        name="out_proj_ffn",
    )(x2, a, b, wo, gf, wg, wu, wd, gn)


def _l1_mixer_kernel(x_ref, g_ref, w_ref, pw_ref, ps_ref, dw_ref, db_ref, cg_ref, cb_ref,
                     c_ref, d_ref, ext_p, ext_u, shift_ref, conv_ref):
    ti = pl.program_id(1)
    tm = x_ref.shape[0]

    @pl.when(ti == 0)
    def _():
        ext_p[:HALO, :] = jnp.zeros((HALO, POOL_WIDTH), F32)
        ext_u[:HALO, :] = jnp.zeros((HALO, CONV_WIDTH), F32)

    h = _rms(x_ref[...], g_ref[...]).astype(BF16)
    xp = _dot(h, w_ref[:, :POOL_WIDTH])
    xa = _dot(h, w_ref[:, POOL_WIDTH:POOL_WIDTH + CONV_WIDTH])
    xg = _dot(h, w_ref[:, POOL_WIDTH + CONV_WIDTH:])
    ext_p[HALO:, :] = xp
    ext_u[HALO:, :] = xa * _sigmoid(xg)

    pos1 = (ti * tm + 1 + lax.broadcasted_iota(jnp.int32, (tm, 1), 0)).astype(F32)
    for g, win in enumerate(POOL_WINDOWS):
        cols = slice(g * POOL_GROUP_DIM, (g + 1) * POOL_GROUP_DIM)
        total = ext_p[HALO:, cols]
        for back in range(1, win):
            total = total + ext_p[HALO - back:HALO - back + tm, cols]
        pooled = total / jnp.minimum(pos1, float(win)) - ext_p[HALO:, cols]
        mixed = _dot(pooled.astype(BF16), pw_ref[g])
        c_ref[:, cols] = (mixed * ps_ref[:, cols]).astype(c_ref.dtype)

    for b in range(1, SUBLANES):
        shift_ref[b - 1, SUBLANES:, :] = ext_u[SUBLANES - b:HALO + tm - b, :]

    def row_block(rb, carry):
        r0 = pl.multiple_of(rb * CONV_ROWS, CONV_ROWS)
        conv = jnp.zeros((CONV_ROWS, CONV_WIDTH), F32) + db_ref[...]
        for back in range(CONV_KERNEL):
            a, b = divmod(back, SUBLANES)
            src = ext_u if b == 0 else shift_ref.at[b - 1]
            tap = CONV_KERNEL - 1 - back
            conv = conv + (src[pl.ds(r0 + HALO - SUBLANES * a, CONV_ROWS), :]
                           * jnp.concatenate([dw_ref[tap]] * (CONV_ROWS // SUBLANES), axis=0))
        conv_ref[pl.ds(r0, CONV_ROWS), :] = conv
        return carry

    lax.fori_loop(0, tm // CONV_ROWS, row_block, 0)
    conv = conv_ref[...]
    mu = jnp.mean(conv, axis=-1, keepdims=True)
    cen = conv - mu
    var = jnp.mean(cen * cen, axis=-1, keepdims=True)
    y = cen * lax.rsqrt(var + EPS) * cg_ref[...] + cb_ref[...]
    d_ref[...] = (y * _sigmoid(y)).astype(d_ref.dtype)

    ext_p[:HALO, :] = ext_p[tm:, :]
    ext_u[:HALO, :] = ext_u[tm:, :]


def _l1_mixer(x2, g, w, pw, ps, dw, db, cg, cb, batch, seq, tm):
    n = x2.shape[0]
    nt = seq // tm
    row = lambda b, t: (b * nt + t, 0)
    return pl.pallas_call(
        _l1_mixer_kernel,
        grid=(batch, nt),
        in_specs=[
            pl.BlockSpec((tm, D_MODEL), row),
            _const_spec((1, D_MODEL)),
            _const_spec((D_MODEL, ODD_IN_WIDTH)),
            _const_spec((len(POOL_WINDOWS), POOL_GROUP_DIM, POOL_GROUP_DIM)),
            _const_spec((1, POOL_WIDTH)),
            _const_spec((CONV_KERNEL, SUBLANES, CONV_WIDTH)),
            _const_spec((1, CONV_WIDTH)),
            _const_spec((1, CONV_WIDTH)),
            _const_spec((1, CONV_WIDTH)),
        ],
        out_specs=[pl.BlockSpec((tm, POOL_WIDTH), row), pl.BlockSpec((tm, CONV_WIDTH), row)],
        out_shape=[jax.ShapeDtypeStruct((n, POOL_WIDTH), BF16),
                   jax.ShapeDtypeStruct((n, CONV_WIDTH), BF16)],
        scratch_shapes=[pltpu.VMEM((HALO + tm, POOL_WIDTH), F32),
                        pltpu.VMEM((HALO + tm, CONV_WIDTH), F32),
                        pltpu.VMEM((SUBLANES - 1, HALO + tm, CONV_WIDTH), F32),
                        pltpu.VMEM((tm, CONV_WIDTH), F32)],
        compiler_params=_params("arbitrary", "arbitrary"),
        name="l1_mixer",
    )(x2, g, w, pw, ps, dw, db, cg, cb)


def _rope_tables(seq):
    half = HEAD_DIM // 2
    inv = jnp.power(ROPE_THETA, -jnp.arange(0, HEAD_DIM, 2, dtype=F32) / HEAD_DIM)
    ang = jnp.arange(seq, dtype=jnp.int32).astype(F32)[:, None] * inv[None, :]
    cos = jnp.cos(ang)
    sin = jnp.sin(ang)
    reps = LANES // HEAD_DIM
    return (jnp.tile(jnp.concatenate([cos, cos], axis=1), (1, reps)),
            jnp.tile(jnp.concatenate([-sin, sin], axis=1), (1, reps)))


def _tile(seq, want):
    return want if seq % want == 0 else seq


def kernel(x, mix_norm_0, w_in_0, lambda_q1_0, lambda_k1_0, lambda_q2_0, lambda_k2_0, subln_0, w_out_0, ffn_norm_0, w_gate_0, w_up_0, w_down_0, mix_norm_1, w_in_1, pool_w_1, pool_scale_1, dw_w_1, dw_b_1, conv_norm_g_1, conv_norm_b_1, w_out_1, ffn_norm_1, w_gate_1, w_up_1, w_down_1, final_norm):
    batch, seq, d = x.shape
    assert d == D_MODEL and seq % LANES == 0
    x2 = x.reshape(batch * seq, d)
    row = lambda v: v.reshape(1, -1).astype(F32)
    tm = _tile(seq, 512)
    tm_ffn = _tile(seq, 1024)

    cos_t, sin_t = _rope_tables(seq)
    proj, svt, dvt = _in_proj0(x2, row(mix_norm_0), w_in_0.astype(BF16), cos_t, sin_t,
                               batch, seq, tm)
    a_out = _sb_attention(proj, svt, batch, seq, _tile(seq, 256))
    lambda_init = 0.8 - 0.6 * math.exp(-0.3 * 0)
    b_out = _diff_attention(proj, dvt, row(lambda_q1_0), row(lambda_k1_0), row(lambda_q2_0),
                            row(lambda_k2_0), row(subln_0), batch, seq, _tile(seq, 512),
                            lambda_init)
    x2 = _out_ffn(x2, a_out, b_out, w_out_0.astype(BF16), row(ffn_norm_0),
                  w_gate_0.astype(BF16), w_up_0.astype(BF16), w_down_0.astype(BF16),
                  row(final_norm), tm_ffn, final_norm=False)

    c_out, d_out = _l1_mixer(x2, row(mix_norm_1), w_in_1.astype(BF16), pool_w_1.astype(BF16),
                             row(pool_scale_1),
                             jnp.broadcast_to(dw_w_1.astype(F32)[:, None, :],
                                              (CONV_KERNEL, SUBLANES, CONV_WIDTH)),
                             row(dw_b_1),
                             row(conv_norm_g_1), row(conv_norm_b_1), batch, seq, tm)
    out = _out_ffn(x2, c_out, d_out, w_out_1.astype(BF16), row(ffn_norm_1),
                   w_gate_1.astype(BF16), w_up_1.astype(BF16), w_down_1.astype(BF16),
                   row(final_norm), tm_ffn, final_norm=True)
    return out.reshape(batch, seq, d)
```

```python
import functools
import math

import jax
import jax.numpy as jnp
from jax import lax
from jax.experimental import pallas as pl
from jax.experimental.pallas import tpu as pltpu

F32 = jnp.float32
BF16 = jnp.bfloat16

D_MODEL = 1024
HEAD_DIM = 64
LANES = 128
SUBLANES = 8
CONV_ROWS = 64
SB_HEADS = 8
DIFF_HEADS = 4
SB_WIDTH = SB_HEADS * HEAD_DIM
DIFF_WIDTH = DIFF_HEADS * 2 * HEAD_DIM
EVEN_IN_WIDTH = 3 * SB_WIDTH + 3 * DIFF_WIDTH
CHUNK = 64
POOL_WINDOWS = (2, 4, 8, 16)
POOL_WIDTH = 512
POOL_GROUP_DIM = 128
CONV_WIDTH = 512
CONV_KERNEL = 31
ODD_IN_WIDTH = POOL_WIDTH + 2 * CONV_WIDTH
D_FF = 2816
FF_CHUNK = 512
ROPE_THETA = 10000.0
EPS = 1e-6
NEG = -1e30
SB_SCALE = HEAD_DIM ** -0.5
LOG2E = math.log2(math.e)
ONES_ROWS = 16
HALO = 32
QK_WIDTH = 2 * SB_WIDTH + 2 * DIFF_WIDTH
SB_LOG2_ZERO = -150.0 * math.log2(math.e)
F32_EXP2_MAX = 126.0
VMEM_LIMIT = 56 * 1024 * 1024


def _rms(x, g):
    return x * lax.rsqrt(jnp.mean(x * x, axis=-1, keepdims=True) + EPS) * g


def _sigmoid(x):
    return 1.0 / (1.0 + jnp.exp(-x))


def _dot(a, b):
    return jnp.dot(a, b, preferred_element_type=F32)


def _dot_nt(a, b):
    return lax.dot_general(a, b, (((1,), (1,)), ((), ())), preferred_element_type=F32)


def _params(*sem):
    return pltpu.CompilerParams(dimension_semantics=sem, vmem_limit_bytes=VMEM_LIMIT)


def _const_spec(shape, single_buffer=False):
    nd = len(shape)
    if single_buffer:
        return pl.BlockSpec(shape, lambda *_: (0,) * nd, pipeline_mode=pl.Buffered(1))
    return pl.BlockSpec(shape, lambda *_: (0,) * nd)


def _in_proj0_kernel(x_ref, g_ref, w_ref, cos_ref, sin_ref, o_ref, svt_ref, dvt_ref):
    h = _rms(x_ref[...], g_ref[...]).astype(BF16)
    tm = h.shape[0]
    lane = lax.broadcasted_iota(jnp.int32, (tm, LANES), 1)
    first_half = (lane % HEAD_DIM) < (HEAD_DIM // 2)
    cos = cos_ref[...]
    sin = sin_ref[...]
    seg = SB_WIDTH
    q_scale = SB_SCALE * LOG2E
    out_col = {0: 0, 1: 1, 3: 2, 4: 3}
    for c in range(EVEN_IN_WIDTH // seg):
        y = _dot(h, w_ref[:, c * seg:(c + 1) * seg])
        for s in range(seg // LANES):
            t = y[:, s * LANES:(s + 1) * LANES]
            if c in (2, 5):
                (svt_ref if c == 2 else dvt_ref)[0, s] = t.T.astype(BF16)
                continue
            if c in (3, 4):
                swapped = jnp.where(first_half,
                                    pltpu.roll(t, LANES - HEAD_DIM // 2, 1),
                                    pltpu.roll(t, HEAD_DIM // 2, 1))
                t = t * cos + swapped * sin
            if c in (0, 3):
                t = t * q_scale
            col = out_col[c] * seg + s * LANES
            o_ref[:, col:col + LANES] = t.astype(BF16)


def _in_proj0(x2, g, w, cos_t, sin_t, batch, seq, tm):
    n = x2.shape[0]
    per_seq = seq // tm
    vt_spec = pl.BlockSpec((1, SB_WIDTH // LANES, LANES, tm),
                           lambda i: (i // per_seq, 0, 0, i % per_seq))
    vt_shape = jax.ShapeDtypeStruct((batch, SB_WIDTH // LANES, LANES, seq), BF16)
    return pl.pallas_call(
        _in_proj0_kernel,
        grid=(n // tm,),
        in_specs=[
            pl.BlockSpec((tm, D_MODEL), lambda i: (i, 0)),
            _const_spec((1, D_MODEL)),
            _const_spec((D_MODEL, EVEN_IN_WIDTH)),
            pl.BlockSpec((tm, LANES), lambda i: (i % per_seq, 0)),
            pl.BlockSpec((tm, LANES), lambda i: (i % per_seq, 0)),
        ],
        out_specs=[pl.BlockSpec((tm, QK_WIDTH), lambda i: (i, 0)), vt_spec, vt_spec],
        out_shape=[jax.ShapeDtypeStruct((n, QK_WIDTH), BF16), vt_shape, vt_shape],
        compiler_params=_params("arbitrary"),
        name="l0_in_proj",
    )(x2, g, w, cos_t, sin_t)


def _log2_one_minus_beta(z):
    return -jnp.maximum(z, jnp.log2(1.0 + jnp.exp2(jnp.minimum(z, F32_EXP2_MAX))))


def _sb_kernel(q_ref, k_ref, vt_ref, o_ref, acc_ref, rest_ref, zl_ref, hi_ref, lo_ref, *, tq):
    lane = lax.broadcasted_iota(jnp.int32, (tq, LANES), 1)
    low = lane < HEAD_DIM
    tri_row = lax.broadcasted_iota(jnp.int32, (tq + ONES_ROWS, tq), 0)
    tri_col = lax.broadcasted_iota(jnp.int32, (tq + ONES_ROWS, tq), 1)
    tri = jnp.logical_or(tri_col > tri_row, tri_row >= tq).astype(BF16)
    head0_rows = lax.broadcasted_iota(jnp.int32, (LANES, tq), 0) < HEAD_DIM

    def sweep(q2, first, nblk, diagonal):
        nk = nblk * tq
        start = pl.multiple_of(first * tq, tq)
        z = _dot_nt(k_ref[pl.ds(start, nk), :], q2)
        log_1m = _log2_one_minus_beta(z)
        if diagonal:
            key = lax.broadcasted_iota(jnp.int32, (nk, 2 * tq), 0) - (nk - tq)
            qry = jnp.bitwise_and(lax.broadcasted_iota(jnp.int32, (nk, 2 * tq), 1), tq - 1)
            strict = key < qry
            log_1m = jnp.where(strict, log_1m, 0.0)
        hi = log_1m.astype(BF16)
        lo = (log_1m - hi.astype(F32)).astype(BF16)
        rest = rest_ref[...]
        log_w = []
        for b in reversed(range(nblk)):
            rows = slice(b * tq, (b + 1) * tq)
            after = _dot(tri, hi[rows]) + _dot(tri, lo[rows])
            log_w.append(z[rows] + log_1m[rows] + after[:tq] + rest)
            rest = rest + after[tq:tq + 1]
        w = jnp.exp2(jnp.concatenate(log_w[::-1], axis=0))
        if diagonal:
            w = jnp.where(strict, w, 0.0)
        acc_ref[...] += _dot(vt_ref[:, pl.ds(start, nk)], w.astype(BF16))
        rest_ref[...] = rest

    def masked_queries(i):
        q = q_ref[pl.ds(pl.multiple_of(i * tq, tq), tq), :]
        zero = jnp.zeros_like(q)
        return jnp.concatenate([jnp.where(low, q, zero), jnp.where(low, zero, q)], axis=0)

    def finish(i, next_block):
        def cond(st):
            j, worst = st
            return jnp.logical_and(j >= 0, worst > SB_LOG2_ZERO)

        def body(st):
            j, _ = st
            sweep(masked_queries(i), j, 1, False)
            return j - 1, jnp.max(rest_ref[...])

        lax.while_loop(cond, body, (next_block, jnp.max(rest_ref[...])))
        acc = acc_ref[...]
        o_ref[pl.ds(pl.multiple_of(i * tq, tq), tq), :] = jnp.where(
            head0_rows, acc[:, :tq], acc[:, tq:]).T.astype(o_ref.dtype)

    def stage_a(i, buf):
        start = pl.multiple_of((i - 1) * tq, tq)
        z = _dot_nt(k_ref[pl.ds(start, 2 * tq), :], masked_queries(i))
        log_1m = _log2_one_minus_beta(z)
        key = lax.broadcasted_iota(jnp.int32, (tq, 2 * tq), 0)
        qry = jnp.bitwise_and(lax.broadcasted_iota(jnp.int32, (tq, 2 * tq), 1), tq - 1)
        strict = key < qry
        for b in range(2):
            rows = slice(b * tq, (b + 1) * tq)
            lm = log_1m[rows]
            zl = z[rows] + lm
            if b == 1:
                lm = jnp.where(strict, lm, 0.0)
                zl = jnp.where(strict, zl, NEG)
            hi = lm.astype(BF16)
            hi_ref[buf, rows, :] = hi
            lo_ref[buf, rows, :] = (lm - hi.astype(F32)).astype(BF16)
            zl_ref[buf, rows, :] = zl

    def stage_b(i, buf):
        rest = jnp.zeros((1, 2 * tq), F32)
        log_w = []
        for b in (1, 0):
            rows = slice(b * tq, (b + 1) * tq)
            after = _dot(tri, hi_ref[buf, rows, :]) + _dot(tri, lo_ref[buf, rows, :])
            log_w.append(zl_ref[buf, rows, :] + after[:tq] + rest)
            rest = rest + after[tq:tq + 1]
        w = jnp.exp2(jnp.concatenate(log_w[::-1], axis=0)).astype(BF16)
        start = pl.multiple_of((i - 1) * tq, tq)
        acc_ref[...] = _dot(vt_ref[:, pl.ds(start, 2 * tq)], w)
        rest_ref[...] = rest
        finish(i, i - 2)

    acc_ref[...] = jnp.zeros_like(acc_ref)
    rest_ref[...] = jnp.zeros_like(rest_ref)
    sweep(masked_queries(0), 0, 1, True)
    finish(0, -1)

    nq = q_ref.shape[0] // tq
    if nq > 1:
        stage_a(1, 1)

        def body(t, carry):
            i = 2 * t + 1
            stage_a(i + 1, 0)
            stage_b(i, 1)
            stage_a(i + 2, 1)
            stage_b(i + 1, 0)
            return carry

        lax.fori_loop(0, nq // 2 - 1, body, 0)
        stage_b(nq - 1, 1)


def _sb_attention(proj, vt, batch, seq, tq):
    n = proj.shape[0]
    nq = seq // tq
    assert tq & (tq - 1) == 0 and (nq == 1 or nq % 2 == 0)
    k_off = SB_WIDTH // LANES
    return pl.pallas_call(
        functools.partial(_sb_kernel, tq=tq),
        grid=(batch, SB_WIDTH // LANES),
        in_specs=[
            pl.BlockSpec((seq, LANES), lambda b, p: (b, p)),
            pl.BlockSpec((seq, LANES), lambda b, p: (b, k_off + p)),
            pl.BlockSpec((None, None, LANES, seq), lambda b, p: (b, p, 0, 0)),
        ],
        out_specs=pl.BlockSpec((seq, LANES), lambda b, p: (b, p)),
        out_shape=jax.ShapeDtypeStruct((n, SB_WIDTH), BF16),
        scratch_shapes=[pltpu.VMEM((LANES, 2 * tq), F32), pltpu.VMEM((1, 2 * tq), F32),
                        pltpu.VMEM((2, 2 * tq, 2 * tq), F32),
                        pltpu.VMEM((2, 2 * tq, 2 * tq), BF16),
                        pltpu.VMEM((2, 2 * tq, 2 * tq), BF16)],
        compiler_params=_params("arbitrary", "arbitrary"),
        name="sb_attention",
    )(proj, proj, vt)


def _diff_kernel(lq1_ref, lk1_ref, lq2_ref, lk2_ref, g_ref, q_ref, k_ref, vt_ref, o_ref,
                 m_ref, acc_ref, s_ref, bmax_ref, *, tq, lambda_init):
    lane = lax.broadcasted_iota(jnp.int32, (tq, LANES), 1)
    low = lane < HEAD_DIM
    ones = jnp.ones((ONES_ROWS, tq), BF16)
    lam = (jnp.exp(jnp.sum(lq1_ref[...] * lk1_ref[...], axis=1, keepdims=True))
           - jnp.exp(jnp.sum(lq2_ref[...] * lk2_ref[...], axis=1, keepdims=True))
           + lambda_init)

    def query_block(i, carry):
        rows = pl.ds(pl.multiple_of(i * tq, tq), tq)
        q = q_ref[rows, :]
        zero = jnp.zeros_like(q)
        q_maps = (jnp.where(low, q, zero), jnp.where(low, zero, q))
        m_ref[...] = jnp.full_like(m_ref, NEG)
        acc_ref[...] = jnp.zeros_like(acc_ref)

        def scores(j, buf, diagonal=False):
            k = k_ref[pl.ds(pl.multiple_of(j * tq, tq), tq), :]
            for mp in range(2):
                s = _dot_nt(k, q_maps[mp])
                if diagonal:
                    visible = ((lax.broadcasted_iota(jnp.int32, (tq, tq), 0) // CHUNK)
                               <= (lax.broadcasted_iota(jnp.int32, (tq, tq), 1) // CHUNK))
                    s = jnp.where(visible, s, NEG)
                s_ref[buf, mp] = s
                bmax_ref[buf, mp] = jnp.max(s, axis=0, keepdims=True)

        def accumulate(j, buf):
            vt = jnp.concatenate([vt_ref[:, pl.ds(pl.multiple_of(j * tq, tq), tq)], ones],
                                 axis=0)
            for mp in range(2):
                m_old = m_ref[mp]
                m_new = jnp.maximum(m_old, bmax_ref[buf, mp])
                p = jnp.exp2(s_ref[buf, mp] - m_new).astype(BF16)
                acc_ref[mp] = jnp.exp2(m_old - m_new) * acc_ref[mp] + _dot(vt, p)
                m_ref[mp] = m_new

        scores(i, 0, diagonal=True)
        pairs = i // 2

        def pair(jj):
            scores(2 * jj, 1)
            accumulate(jnp.where(jj == 0, i, 2 * jj - 1), 0)
            scores(2 * jj + 1, 0)
            accumulate(2 * jj, 1)

        def pairs_per_trip(n):
            def trip(t, c):
                for u in range(n):
                    pair(n * t + u)
                return c
            return trip

        lax.fori_loop(0, pairs // 4, pairs_per_trip(4), 0)
        lax.fori_loop(2 * (pairs // 4), pairs // 2, pairs_per_trip(2), 0)
        lax.fori_loop(2 * (pairs // 2), pairs, pairs_per_trip(1), 0)
        held = jnp.where(pairs == 0, i, 2 * pairs - 1)

        @pl.when(i % 2 == 1)
        def _():
            scores(i - 1, 1)
            accumulate(held, 0)
            accumulate(i - 1, 1)

        @pl.when(i % 2 == 0)
        def _():
            accumulate(held, 0)

        acc1 = acc_ref[0]
        acc2 = acc_ref[1]
        out_t = (acc1[:LANES] * (1.0 / acc1[LANES:LANES + 1])
                 - lam * (acc2[:LANES] * (1.0 / acc2[LANES:LANES + 1])))
        out = _rms(out_t.T, g_ref[...]) * (1.0 - lambda_init)
        o_ref[rows, :] = out.astype(o_ref.dtype)
        return carry

    lax.fori_loop(0, q_ref.shape[0] // tq, query_block, 0)


def _diff_attention(proj, vt, lq1, lk1, lq2, lk2, subln_g, batch, seq, tq, lambda_init):
    n = proj.shape[0]
    q_off = 2 * SB_WIDTH // LANES
    k_off = q_off + DIFF_WIDTH // LANES
    vec = _const_spec((1, HEAD_DIM))
    return pl.pallas_call(
        functools.partial(_diff_kernel, tq=tq, lambda_init=lambda_init),
        grid=(batch, DIFF_HEADS),
        in_specs=[
            vec, vec, vec, vec,
            _const_spec((1, LANES)),
            pl.BlockSpec((seq, LANES), lambda b, h: (b, q_off + h)),
            pl.BlockSpec((seq, LANES), lambda b, h: (b, k_off + h)),
            pl.BlockSpec((None, None, LANES, seq), lambda b, h: (b, h, 0, 0)),
        ],
        out_specs=pl.BlockSpec((seq, LANES), lambda b, h: (b, h)),
        out_shape=jax.ShapeDtypeStruct((n, DIFF_WIDTH), BF16),
        scratch_shapes=[pltpu.VMEM((2, 1, tq), F32),
                        pltpu.VMEM((2, LANES + ONES_ROWS, tq), F32),
                        pltpu.VMEM((2, 2, tq, tq), F32),
                        pltpu.VMEM((2, 2, 1, tq), F32)],
        compiler_params=_params("arbitrary", "arbitrary"),
        name="diff_attention",
    )(lq1, lk1, lq2, lk2, subln_g, proj, proj, vt)


def _out_ffn_kernel(x_ref, a_ref, b_ref, wo_ref, gf_ref, wg_ref, wu_ref, wd_ref, gn_ref, o_ref,
                    acc_ref, *, final_norm):
    half = a_ref.shape[1]
    x1 = x_ref[...] + _dot(a_ref[...], wo_ref[:half, :]) + _dot(b_ref[...], wo_ref[half:, :])
    h = _rms(x1, gf_ref[...]).astype(BF16)
    acc_ref[...] = x1

    def fold(cols):
        gate = _dot(h, wg_ref[:, cols])
        act = (gate * _sigmoid(gate) * _dot(h, wu_ref[:, cols])).astype(BF16)
        acc_ref[...] += _dot(act, wd_ref[cols, :])

    def body(c, carry):
        fold(pl.ds(pl.multiple_of(c * FF_CHUNK, FF_CHUNK), FF_CHUNK))
        return carry

    full = D_FF // FF_CHUNK
    lax.fori_loop(0, full, body, 0)
    if D_FF % FF_CHUNK:
        fold(slice(full * FF_CHUNK, D_FF))
    y = acc_ref[...]
    if final_norm:
        y = _rms(y, gn_ref[...])
    o_ref[...] = y


def _out_ffn(x2, a, b, wo, gf, wg, wu, wd, gn, tm, final_norm):
    n = x2.shape[0]
    half = a.shape[1]
    return pl.pallas_call(
        functools.partial(_out_ffn_kernel, final_norm=final_norm),
        grid=(n // tm,),
        in_specs=[
            pl.BlockSpec((tm, D_MODEL), lambda i: (i, 0)),
            pl.BlockSpec((tm, half), lambda i: (i, 0)),
            pl.BlockSpec((tm, half), lambda i: (i, 0)),
            _const_spec((2 * half, D_MODEL), single_buffer=True),
            _const_spec((1, D_MODEL)),
            _const_spec((D_MODEL, D_FF), single_buffer=True),
            _const_spec((D_MODEL, D_FF), single_buffer=True),
            _const_spec((D_FF, D_MODEL), single_buffer=True),
            _const_spec((1, D_MODEL)),
        ],
        out_specs=pl.BlockSpec((tm, D_MODEL), lambda i: (i, 0)),
        out_shape=jax.ShapeDtypeStruct((n, D_MODEL), F32),
        scratch_shapes=[pltpu.VMEM((tm, D_MODEL), F32)],
        compiler_params=_params("arbitrary"),
        name="out_proj_ffn",
    )(x2, a, b, wo, gf, wg, wu, wd, gn)


def _l1_mixer_kernel(x_ref, g_ref, w_ref, pw_ref, ps_ref, dw_ref, db_ref, cg_ref, cb_ref,
                     c_ref, d_ref, ext_p, ext_u, shift_ref, conv_ref):
    ti = pl.program_id(1)
    tm = x_ref.shape[0]

    @pl.when(ti == 0)
    def _():
        ext_p[:HALO, :] = jnp.zeros((HALO, POOL_WIDTH), F32)
        ext_u[:HALO, :] = jnp.zeros((HALO, CONV_WIDTH), F32)

    h = _rms(x_ref[...], g_ref[...]).astype(BF16)
    xp = _dot(h, w_ref[:, :POOL_WIDTH])
    xa = _dot(h, w_ref[:, POOL_WIDTH:POOL_WIDTH + CONV_WIDTH])
    xg = _dot(h, w_ref[:, POOL_WIDTH + CONV_WIDTH:])
    ext_p[HALO:, :] = xp
    ext_u[HALO:, :] = xa * _sigmoid(xg)

    pos1 = (ti * tm + 1 + lax.broadcasted_iota(jnp.int32, (tm, 1), 0)).astype(F32)
    for g, win in enumerate(POOL_WINDOWS):
        cols = slice(g * POOL_GROUP_DIM, (g + 1) * POOL_GROUP_DIM)
        total = ext_p[HALO:, cols]
        for back in range(1, win):
            total = total + ext_p[HALO - back:HALO - back + tm, cols]
        pooled = total / jnp.minimum(pos1, float(win)) - ext_p[HALO:, cols]
        mixed = _dot(pooled.astype(BF16), pw_ref[g])
        c_ref[:, cols] = (mixed * ps_ref[:, cols]).astype(c_ref.dtype)

    for b in range(1, SUBLANES):
        shift_ref[b - 1, SUBLANES:, :] = ext_u[SUBLANES - b:HALO + tm - b, :]

    def row_block(rb, carry):
        r0 = pl.multiple_of(rb * CONV_ROWS, CONV_ROWS)
        conv = jnp.zeros((CONV_ROWS, CONV_WIDTH), F32) + db_ref[...]
        for back in range(CONV_KERNEL):
            a, b = divmod(back, SUBLANES)
            src = ext_u if b == 0 else shift_ref.at[b - 1]
            tap = CONV_KERNEL - 1 - back
            conv = conv + (src[pl.ds(r0 + HALO - SUBLANES * a, CONV_ROWS), :]
                           * jnp.concatenate([dw_ref[tap]] * (CONV_ROWS // SUBLANES), axis=0))
        conv_ref[pl.ds(r0, CONV_ROWS), :] = conv
        return carry

    lax.fori_loop(0, tm // CONV_ROWS, row_block, 0)
    conv = conv_ref[...]
    mu = jnp.mean(conv, axis=-1, keepdims=True)
    cen = conv - mu
    var = jnp.mean(cen * cen, axis=-1, keepdims=True)
    y = cen * lax.rsqrt(var + EPS) * cg_ref[...] + cb_ref[...]
    d_ref[...] = (y * _sigmoid(y)).astype(d_ref.dtype)

    ext_p[:HALO, :] = ext_p[tm:, :]
    ext_u[:HALO, :] = ext_u[tm:, :]


def _l1_mixer(x2, g, w, pw, ps, dw, db, cg, cb, batch, seq, tm):
    n = x2.shape[0]
    nt = seq // tm
    row = lambda b, t: (b * nt + t, 0)
    return pl.pallas_call(
        _l1_mixer_kernel,
        grid=(batch, nt),
        in_specs=[
            pl.BlockSpec((tm, D_MODEL), row),
            _const_spec((1, D_MODEL)),
            _const_spec((D_MODEL, ODD_IN_WIDTH)),
            _const_spec((len(POOL_WINDOWS), POOL_GROUP_DIM, POOL_GROUP_DIM)),
            _const_spec((1, POOL_WIDTH)),
            _const_spec((CONV_KERNEL, SUBLANES, CONV_WIDTH)),
            _const_spec((1, CONV_WIDTH)),
            _const_spec((1, CONV_WIDTH)),
            _const_spec((1, CONV_WIDTH)),
        ],
        out_specs=[pl.BlockSpec((tm, POOL_WIDTH), row), pl.BlockSpec((tm, CONV_WIDTH), row)],
        out_shape=[jax.ShapeDtypeStruct((n, POOL_WIDTH), BF16),
                   jax.ShapeDtypeStruct((n, CONV_WIDTH), BF16)],
        scratch_shapes=[pltpu.VMEM((HALO + tm, POOL_WIDTH), F32),
                        pltpu.VMEM((HALO + tm, CONV_WIDTH), F32),
                        pltpu.VMEM((SUBLANES - 1, HALO + tm, CONV_WIDTH), F32),
                        pltpu.VMEM((tm, CONV_WIDTH), F32)],
        compiler_params=_params("arbitrary", "arbitrary"),
        name="l1_mixer",
    )(x2, g, w, pw, ps, dw, db, cg, cb)


def _rope_tables(seq):
    half = HEAD_DIM // 2
    inv = jnp.power(ROPE_THETA, -jnp.arange(0, HEAD_DIM, 2, dtype=F32) / HEAD_DIM)
    ang = jnp.arange(seq, dtype=jnp.int32).astype(F32)[:, None] * inv[None, :]
    cos = jnp.cos(ang)
    sin = jnp.sin(ang)
    reps = LANES // HEAD_DIM
    return (jnp.tile(jnp.concatenate([cos, cos], axis=1), (1, reps)),
            jnp.tile(jnp.concatenate([-sin, sin], axis=1), (1, reps)))


def _tile(seq, want):
    return want if seq % want == 0 else seq


def kernel(x, mix_norm_0, w_in_0, lambda_q1_0, lambda_k1_0, lambda_q2_0, lambda_k2_0, subln_0, w_out_0, ffn_norm_0, w_gate_0, w_up_0, w_down_0, mix_norm_1, w_in_1, pool_w_1, pool_scale_1, dw_w_1, dw_b_1, conv_norm_g_1, conv_norm_b_1, w_out_1, ffn_norm_1, w_gate_1, w_up_1, w_down_1, final_norm):
    batch, seq, d = x.shape
    assert d == D_MODEL and seq % LANES == 0
    x2 = x.reshape(batch * seq, d)
    row = lambda v: v.reshape(1, -1).astype(F32)
    tm = _tile(seq, 512)
    tm_ffn = _tile(seq, 1024)

    cos_t, sin_t = _rope_tables(seq)
    proj, svt, dvt = _in_proj0(x2, row(mix_norm_0), w_in_0.astype(BF16), cos_t, sin_t,
                               batch, seq, tm)
    a_out = _sb_attention(proj, svt, batch, seq, _tile(seq, 256))
    lambda_init = 0.8 - 0.6 * math.exp(-0.3 * 0)
    b_out = _diff_attention(proj, dvt, row(lambda_q1_0), row(lambda_k1_0), row(lambda_q2_0),
                            row(lambda_k2_0), row(subln_0), batch, seq, _tile(seq, 512),
                            lambda_init)
    x2 = _out_ffn(x2, a_out, b_out, w_out_0.astype(BF16), row(ffn_norm_0),
                  w_gate_0.astype(BF16), w_up_0.astype(BF16), w_down_0.astype(BF16),
                  row(final_norm), tm_ffn, final_norm=False)

    c_out, d_out = _l1_mixer(x2, row(mix_norm_1), w_in_1.astype(BF16), pool_w_1.astype(BF16),
                             row(pool_scale_1),
                             jnp.broadcast_to(dw_w_1.astype(F32)[:, None, :],
                                              (CONV_KERNEL, SUBLANES, CONV_WIDTH)),
                             row(dw_b_1),
                             row(conv_norm_g_1), row(conv_norm_b_1), batch, seq, tm)
    out = _out_ffn(x2, c_out, d_out, w_out_1.astype(BF16), row(ffn_norm_1),
                   w_gate_1.astype(BF16), w_up_1.astype(BF16), w_down_1.astype(BF16),
                   row(final_norm), tm_ffn, final_norm=True)
    return out.reshape(batch, seq, d)
```

```python
import functools
import math

import jax
import jax.numpy as jnp
from jax import lax
from jax.experimental import pallas as pl
from jax.experimental.pallas import tpu as pltpu

F32 = jnp.float32
BF16 = jnp.bfloat16

D_MODEL = 1024
HEAD_DIM = 64
LANES = 128
SUBLANES = 8
CONV_ROWS = 64
SB_HEADS = 8
DIFF_HEADS = 4
SB_WIDTH = SB_HEADS * HEAD_DIM
DIFF_WIDTH = DIFF_HEADS * 2 * HEAD_DIM
EVEN_IN_WIDTH = 3 * SB_WIDTH + 3 * DIFF_WIDTH
CHUNK = 64
POOL_WINDOWS = (2, 4, 8, 16)
POOL_WIDTH = 512
POOL_GROUP_DIM = 128
CONV_WIDTH = 512
CONV_KERNEL = 31
ODD_IN_WIDTH = POOL_WIDTH + 2 * CONV_WIDTH
D_FF = 2816
FF_CHUNK = 512
ROPE_THETA = 10000.0
EPS = 1e-6
NEG = -1e30
SB_SCALE = HEAD_DIM ** -0.5
LOG2E = math.log2(math.e)
ONES_ROWS = 16
HALO = 32
QK_WIDTH = 2 * SB_WIDTH + 2 * DIFF_WIDTH
SB_LOG2_ZERO = -150.0 * math.log2(math.e)
F32_EXP2_MAX = 126.0
VMEM_LIMIT = 56 * 1024 * 1024


def _rms(x, g):
    return x * lax.rsqrt(jnp.mean(x * x, axis=-1, keepdims=True) + EPS) * g


def _sigmoid(x):
    return 1.0 / (1.0 + jnp.exp(-x))


def _dot(a, b):
    return jnp.dot(a, b, preferred_element_type=F32)


def _dot_nt(a, b):
    return lax.dot_general(a, b, (((1,), (1,)), ((), ())), preferred_element_type=F32)


def _params(*sem):
    return pltpu.CompilerParams(dimension_semantics=sem, vmem_limit_bytes=VMEM_LIMIT)


def _const_spec(shape, single_buffer=False):
    nd = len(shape)
    if single_buffer:
        return pl.BlockSpec(shape, lambda *_: (0,) * nd, pipeline_mode=pl.Buffered(1))
    return pl.BlockSpec(shape, lambda *_: (0,) * nd)


def _in_proj0_kernel(x_ref, g_ref, w_ref, cos_ref, sin_ref, o_ref, svt_ref, dvt_ref):
    h = _rms(x_ref[...], g_ref[...]).astype(BF16)
    tm = h.shape[0]
    lane = lax.broadcasted_iota(jnp.int32, (tm, LANES), 1)
    first_half = (lane % HEAD_DIM) < (HEAD_DIM // 2)
    cos = cos_ref[...]
    sin = sin_ref[...]
    seg = SB_WIDTH
    q_scale = SB_SCALE * LOG2E
    out_col = {0: 0, 1: 1, 3: 2, 4: 3}
    for c in range(EVEN_IN_WIDTH // seg):
        y = _dot(h, w_ref[:, c * seg:(c + 1) * seg])
        for s in range(seg // LANES):
            t = y[:, s * LANES:(s + 1) * LANES]
            if c in (2, 5):
                (svt_ref if c == 2 else dvt_ref)[0, s] = t.T.astype(BF16)
                continue
            if c in (3, 4):
                swapped = jnp.where(first_half,
                                    pltpu.roll(t, LANES - HEAD_DIM // 2, 1),
                                    pltpu.roll(t, HEAD_DIM // 2, 1))
                t = t * cos + swapped * sin
            if c in (0, 3):
                t = t * q_scale
            col = out_col[c] * seg + s * LANES
            o_ref[:, col:col + LANES] = t.astype(BF16)


def _in_proj0(x2, g, w, cos_t, sin_t, batch, seq, tm):
    n = x2.shape[0]
    per_seq = seq // tm
    vt_spec = pl.BlockSpec((1, SB_WIDTH // LANES, LANES, tm),
                           lambda i: (i // per_seq, 0, 0, i % per_seq))
    vt_shape = jax.ShapeDtypeStruct((batch, SB_WIDTH // LANES, LANES, seq), BF16)
    return pl.pallas_call(
        _in_proj0_kernel,
        grid=(n // tm,),
        in_specs=[
            pl.BlockSpec((tm, D_MODEL), lambda i: (i, 0)),
            _const_spec((1, D_MODEL)),
            _const_spec((D_MODEL, EVEN_IN_WIDTH)),
            pl.BlockSpec((tm, LANES), lambda i: (i % per_seq, 0)),
            pl.BlockSpec((tm, LANES), lambda i: (i % per_seq, 0)),
        ],
        out_specs=[pl.BlockSpec((tm, QK_WIDTH), lambda i: (i, 0)), vt_spec, vt_spec],
        out_shape=[jax.ShapeDtypeStruct((n, QK_WIDTH), BF16), vt_shape, vt_shape],
        compiler_params=_params("arbitrary"),
        name="l0_in_proj",
    )(x2, g, w, cos_t, sin_t)


def _log2_one_minus_beta(z):
    return -jnp.maximum(z, jnp.log2(1.0 + jnp.exp2(jnp.minimum(z, F32_EXP2_MAX))))


def _sb_kernel(q_ref, k_ref, vt_ref, o_ref, acc_ref, rest_ref, zl_ref, hi_ref, lo_ref, *, tq):
    lane = lax.broadcasted_iota(jnp.int32, (tq, LANES), 1)
    low = lane < HEAD_DIM
    tri_row = lax.broadcasted_iota(jnp.int32, (tq + ONES_ROWS, tq), 0)
    tri_col = lax.broadcasted_iota(jnp.int32, (tq + ONES_ROWS, tq), 1)
    tri = jnp.logical_or(tri_col > tri_row, tri_row >= tq).astype(BF16)
    head0_rows = lax.broadcasted_iota(jnp.int32, (LANES, tq), 0) < HEAD_DIM

    def sweep(q2, first, nblk, diagonal):
        nk = nblk * tq
        start = pl.multiple_of(first * tq, tq)
        z = _dot_nt(k_ref[pl.ds(start, nk), :], q2)
        log_1m = _log2_one_minus_beta(z)
        if diagonal:
            key = lax.broadcasted_iota(jnp.int32, (nk, 2 * tq), 0) - (nk - tq)
            qry = jnp.bitwise_and(lax.broadcasted_iota(jnp.int32, (nk, 2 * tq), 1), tq - 1)
            strict = key < qry
            log_1m = jnp.where(strict, log_1m, 0.0)
        hi = log_1m.astype(BF16)
        lo = (log_1m - hi.astype(F32)).astype(BF16)
        rest = rest_ref[...]
        log_w = []
        for b in reversed(range(nblk)):
            rows = slice(b * tq, (b + 1) * tq)
            after = _dot(tri, hi[rows]) + _dot(tri, lo[rows])
            log_w.append(z[rows] + log_1m[rows] + after[:tq] + rest)
            rest = rest + after[tq:tq + 1]
        w = jnp.exp2(jnp.concatenate(log_w[::-1], axis=0))
        if diagonal:
            w = jnp.where(strict, w, 0.0)
        acc_ref[...] += _dot(vt_ref[:, pl.ds(start, nk)], w.astype(BF16))
        rest_ref[...] = rest

    def masked_queries(i):
        q = q_ref[pl.ds(pl.multiple_of(i * tq, tq), tq), :]
        zero = jnp.zeros_like(q)
        return jnp.concatenate([jnp.where(low, q, zero), jnp.where(low, zero, q)], axis=0)

    def finish(i, next_block):
        def cond(st):
            j, worst = st
            return jnp.logical_and(j >= 0, worst > SB_LOG2_ZERO)

        def body(st):
            j, _ = st
            sweep(masked_queries(i), j, 1, False)
            return j - 1, jnp.max(rest_ref[...])

        lax.while_loop(cond, body, (next_block, jnp.max(rest_ref[...])))
        acc = acc_ref[...]
        o_ref[pl.ds(pl.multiple_of(i * tq, tq), tq), :] = jnp.where(
            head0_rows, acc[:, :tq], acc[:, tq:]).T.astype(o_ref.dtype)

    def stage_a(i, buf):
        start = pl.multiple_of((i - 1) * tq, tq)
        z = _dot_nt(k_ref[pl.ds(start, 2 * tq), :], masked_queries(i))
        log_1m = _log2_one_minus_beta(z)
        key = lax.broadcasted_iota(jnp.int32, (tq, 2 * tq), 0)
        qry = jnp.bitwise_and(lax.broadcasted_iota(jnp.int32, (tq, 2 * tq), 1), tq - 1)
        strict = key < qry
        for b in range(2):
            rows = slice(b * tq, (b + 1) * tq)
            lm = log_1m[rows]
            zl = z[rows] + lm
            if b == 1:
                lm = jnp.where(strict, lm, 0.0)
                zl = jnp.where(strict, zl, NEG)
            hi = lm.astype(BF16)
            hi_ref[buf, rows, :] = hi
            lo_ref[buf, rows, :] = (lm - hi.astype(F32)).astype(BF16)
            zl_ref[buf, rows, :] = zl

    def stage_b(i, buf):
        rest = jnp.zeros((1, 2 * tq), F32)
        log_w = []
        for b in (1, 0):
            rows = slice(b * tq, (b + 1) * tq)
            after = _dot(tri, hi_ref[buf, rows, :]) + _dot(tri, lo_ref[buf, rows, :])
            log_w.append(zl_ref[buf, rows, :] + after[:tq] + rest)
            rest = rest + after[tq:tq + 1]
        w = jnp.exp2(jnp.concatenate(log_w[::-1], axis=0)).astype(BF16)
        start = pl.multiple_of((i - 1) * tq, tq)
        acc_ref[...] = _dot(vt_ref[:, pl.ds(start, 2 * tq)], w)
        rest_ref[...] = rest
        finish(i, i - 2)

    acc_ref[...] = jnp.zeros_like(acc_ref)
    rest_ref[...] = jnp.zeros_like(rest_ref)
    sweep(masked_queries(0), 0, 1, True)
    finish(0, -1)

    nq = q_ref.shape[0] // tq
    if nq > 1:
        stage_a(1, 1)

        def body(t, carry):
            i = 2 * t + 1
            stage_a(i + 1, 0)
            stage_b(i, 1)
            stage_a(i + 2, 1)
            stage_b(i + 1, 0)
            return carry

        lax.fori_loop(0, nq // 2 - 1, body, 0)
        stage_b(nq - 1, 1)


def _sb_attention(proj, vt, batch, seq, tq):
    n = proj.shape[0]
    nq = seq // tq
    assert tq & (tq - 1) == 0 and (nq == 1 or nq % 2 == 0)
    k_off = SB_WIDTH // LANES
    return pl.pallas_call(
        functools.partial(_sb_kernel, tq=tq),
        grid=(batch, SB_WIDTH // LANES),
        in_specs=[
            pl.BlockSpec((seq, LANES), lambda b, p: (b, p)),
            pl.BlockSpec((seq, LANES), lambda b, p: (b, k_off + p)),
            pl.BlockSpec((None, None, LANES, seq), lambda b, p: (b, p, 0, 0)),
        ],
        out_specs=pl.BlockSpec((seq, LANES), lambda b, p: (b, p)),
        out_shape=jax.ShapeDtypeStruct((n, SB_WIDTH), BF16),
        scratch_shapes=[pltpu.VMEM((LANES, 2 * tq), F32), pltpu.VMEM((1, 2 * tq), F32),
                        pltpu.VMEM((2, 2 * tq, 2 * tq), F32),
                        pltpu.VMEM((2, 2 * tq, 2 * tq), BF16),
                        pltpu.VMEM((2, 2 * tq, 2 * tq), BF16)],
        compiler_params=_params("arbitrary", "arbitrary"),
        name="sb_attention",
    )(proj, proj, vt)


def _diff_kernel(lq1_ref, lk1_ref, lq2_ref, lk2_ref, g_ref, q_ref, k_ref, vt_ref, o_ref,
                 m_ref, acc_ref, s_ref, bmax_ref, *, tq, lambda_init):
    lane = lax.broadcasted_iota(jnp.int32, (tq, LANES), 1)
    low = lane < HEAD_DIM
    ones = jnp.ones((ONES_ROWS, tq), BF16)
    lam = (jnp.exp(jnp.sum(lq1_ref[...] * lk1_ref[...], axis=1, keepdims=True))
           - jnp.exp(jnp.sum(lq2_ref[...] * lk2_ref[...], axis=1, keepdims=True))
           + lambda_init)

    def query_block(i, carry):
        rows = pl.ds(pl.multiple_of(i * tq, tq), tq)
        q = q_ref[rows, :]
        zero = jnp.zeros_like(q)
        q_maps = (jnp.where(low, q, zero), jnp.where(low, zero, q))
        m_ref[...] = jnp.full_like(m_ref, NEG)
        acc_ref[...] = jnp.zeros_like(acc_ref)

        def scores(j, buf, diagonal=False):
            k = k_ref[pl.ds(pl.multiple_of(j * tq, tq), tq), :]
            for mp in range(2):
                s = _dot_nt(k, q_maps[mp])
                if diagonal:
                    visible = ((lax.broadcasted_iota(jnp.int32, (tq, tq), 0) // CHUNK)
                               <= (lax.broadcasted_iota(jnp.int32, (tq, tq), 1) // CHUNK))
                    s = jnp.where(visible, s, NEG)
                s_ref[buf, mp] = s
                bmax_ref[buf, mp] = jnp.max(s, axis=0, keepdims=True)

        def accumulate(j, buf):
            vt = jnp.concatenate([vt_ref[:, pl.ds(pl.multiple_of(j * tq, tq), tq)], ones],
                                 axis=0)
            for mp in range(2):
                m_old = m_ref[mp]
                m_new = jnp.maximum(m_old, bmax_ref[buf, mp])
                p = jnp.exp2(s_ref[buf, mp] - m_new).astype(BF16)
                acc_ref[mp] = jnp.exp2(m_old - m_new) * acc_ref[mp] + _dot(vt, p)
                m_ref[mp] = m_new

        scores(i, 0, diagonal=True)
        pairs = i // 2

        def pair(jj):
            scores(2 * jj, 1)
            accumulate(jnp.where(jj == 0, i, 2 * jj - 1), 0)
            scores(2 * jj + 1, 0)
            accumulate(2 * jj, 1)

        def pairs_per_trip(n):
            def trip(t, c):
                for u in range(n):
                    pair(n * t + u)
                return c
            return trip

        lax.fori_loop(0, pairs // 8, pairs_per_trip(8), 0)
        lax.fori_loop(2 * (pairs // 8), pairs // 4, pairs_per_trip(4), 0)
        lax.fori_loop(2 * (pairs // 4), pairs // 2, pairs_per_trip(2), 0)
        lax.fori_loop(2 * (pairs // 2), pairs, pairs_per_trip(1), 0)
        held = jnp.where(pairs == 0, i, 2 * pairs - 1)

        @pl.when(i % 2 == 1)
        def _():
            scores(i - 1, 1)
            accumulate(held, 0)
            accumulate(i - 1, 1)

        @pl.when(i % 2 == 0)
        def _():
            accumulate(held, 0)

        acc1 = acc_ref[0]
        acc2 = acc_ref[1]
        out_t = (acc1[:LANES] * (1.0 / acc1[LANES:LANES + 1])
                 - lam * (acc2[:LANES] * (1.0 / acc2[LANES:LANES + 1])))
        out = _rms(out_t.T, g_ref[...]) * (1.0 - lambda_init)
        o_ref[rows, :] = out.astype(o_ref.dtype)
        return carry

    lax.fori_loop(0, q_ref.shape[0] // tq, query_block, 0)


def _diff_attention(proj, vt, lq1, lk1, lq2, lk2, subln_g, batch, seq, tq, lambda_init):
    n = proj.shape[0]
    q_off = 2 * SB_WIDTH // LANES
    k_off = q_off + DIFF_WIDTH // LANES
    vec = _const_spec((1, HEAD_DIM))
    return pl.pallas_call(
        functools.partial(_diff_kernel, tq=tq, lambda_init=lambda_init),
        grid=(batch, DIFF_HEADS),
        in_specs=[
            vec, vec, vec, vec,
            _const_spec((1, LANES)),
            pl.BlockSpec((seq, LANES), lambda b, h: (b, q_off + h)),
            pl.BlockSpec((seq, LANES), lambda b, h: (b, k_off + h)),
            pl.BlockSpec((None, None, LANES, seq), lambda b, h: (b, h, 0, 0)),
        ],
        out_specs=pl.BlockSpec((seq, LANES), lambda b, h: (b, h)),
        out_shape=jax.ShapeDtypeStruct((n, DIFF_WIDTH), BF16),
        scratch_shapes=[pltpu.VMEM((2, 1, tq), F32),
                        pltpu.VMEM((2, LANES + ONES_ROWS, tq), F32),
                        pltpu.VMEM((2, 2, tq, tq), F32),
                        pltpu.VMEM((2, 2, 1, tq), F32)],
        compiler_params=_params("arbitrary", "arbitrary"),
        name="diff_attention",
    )(lq1, lk1, lq2, lk2, subln_g, proj, proj, vt)


def _out_ffn_kernel(x_ref, a_ref, b_ref, wo_ref, gf_ref, wg_ref, wu_ref, wd_ref, gn_ref, o_ref,
                    acc_ref, *, final_norm):
    half = a_ref.shape[1]
    x1 = x_ref[...] + _dot(a_ref[...], wo_ref[:half, :]) + _dot(b_ref[...], wo_ref[half:, :])
    h = _rms(x1, gf_ref[...]).astype(BF16)
    acc_ref[...] = x1

    def fold(cols):
        gate = _dot(h, wg_ref[:, cols])
        act = (gate * _sigmoid(gate) * _dot(h, wu_ref[:, cols])).astype(BF16)
        acc_ref[...] += _dot(act, wd_ref[cols, :])

    def body(c, carry):
        fold(pl.ds(pl.multiple_of(c * FF_CHUNK, FF_CHUNK), FF_CHUNK))
        return carry

    full = D_FF // FF_CHUNK
    lax.fori_loop(0, full, body, 0)
    if D_FF % FF_CHUNK:
        fold(slice(full * FF_CHUNK, D_FF))
    y = acc_ref[...]
    if final_norm:
        y = _rms(y, gn_ref[...])
    o_ref[...] = y


def _out_ffn(x2, a, b, wo, gf, wg, wu, wd, gn, tm, final_norm):
    n = x2.shape[0]
    half = a.shape[1]
    return pl.pallas_call(
        functools.partial(_out_ffn_kernel, final_norm=final_norm),
        grid=(n // tm,),
        in_specs=[
            pl.BlockSpec((tm, D_MODEL), lambda i: (i, 0)),
            pl.BlockSpec((tm, half), lambda i: (i, 0)),
            pl.BlockSpec((tm, half), lambda i: (i, 0)),
            _const_spec((2 * half, D_MODEL), single_buffer=True),
            _const_spec((1, D_MODEL)),
            _const_spec((D_MODEL, D_FF), single_buffer=True),
            _const_spec((D_MODEL, D_FF), single_buffer=True),
            _const_spec((D_FF, D_MODEL), single_buffer=True),
            _const_spec((1, D_MODEL)),
        ],
        out_specs=pl.BlockSpec((tm, D_MODEL), lambda i: (i, 0)),
        out_shape=jax.ShapeDtypeStruct((n, D_MODEL), F32),
        scratch_shapes=[pltpu.VMEM((tm, D_MODEL), F32)],
        compiler_params=_params("arbitrary"),
        name="out_proj_ffn",
    )(x2, a, b, wo, gf, wg, wu, wd, gn)


def _l1_mixer_kernel(x_ref, g_ref, w_ref, pw_ref, ps_ref, dw_ref, db_ref, cg_ref, cb_ref,
                     c_ref, d_ref, ext_p, ext_u, shift_ref, conv_ref):
    ti = pl.program_id(1)
    tm = x_ref.shape[0]

    @pl.when(ti == 0)
    def _():
        ext_p[:HALO, :] = jnp.zeros((HALO, POOL_WIDTH), F32)
        ext_u[:HALO, :] = jnp.zeros((HALO, CONV_WIDTH), F32)

    h = _rms(x_ref[...], g_ref[...]).astype(BF16)
    xp = _dot(h, w_ref[:, :POOL_WIDTH])
    xa = _dot(h, w_ref[:, POOL_WIDTH:POOL_WIDTH + CONV_WIDTH])
    xg = _dot(h, w_ref[:, POOL_WIDTH + CONV_WIDTH:])
    ext_p[HALO:, :] = xp
    ext_u[HALO:, :] = xa * _sigmoid(xg)

    pos1 = (ti * tm + 1 + lax.broadcasted_iota(jnp.int32, (tm, 1), 0)).astype(F32)
    for g, win in enumerate(POOL_WINDOWS):
        cols = slice(g * POOL_GROUP_DIM, (g + 1) * POOL_GROUP_DIM)
        total = ext_p[HALO:, cols]
        for back in range(1, win):
            total = total + ext_p[HALO - back:HALO - back + tm, cols]
        pooled = total / jnp.minimum(pos1, float(win)) - ext_p[HALO:, cols]
        mixed = _dot(pooled.astype(BF16), pw_ref[g])
        c_ref[:, cols] = (mixed * ps_ref[:, cols]).astype(c_ref.dtype)

    for b in range(1, SUBLANES):
        shift_ref[b - 1, SUBLANES:, :] = ext_u[SUBLANES - b:HALO + tm - b, :]

    def row_block(rb, carry):
        r0 = pl.multiple_of(rb * CONV_ROWS, CONV_ROWS)
        conv = jnp.zeros((CONV_ROWS, CONV_WIDTH), F32) + db_ref[...]
        for back in range(CONV_KERNEL):
            a, b = divmod(back, SUBLANES)
            src = ext_u if b == 0 else shift_ref.at[b - 1]
            tap = CONV_KERNEL - 1 - back
            conv = conv + (src[pl.ds(r0 + HALO - SUBLANES * a, CONV_ROWS), :]
                           * jnp.concatenate([dw_ref[tap]] * (CONV_ROWS // SUBLANES), axis=0))
        conv_ref[pl.ds(r0, CONV_ROWS), :] = conv
        return carry

    lax.fori_loop(0, tm // CONV_ROWS, row_block, 0)
    conv = conv_ref[...]
    mu = jnp.mean(conv, axis=-1, keepdims=True)
    cen = conv - mu
    var = jnp.mean(cen * cen, axis=-1, keepdims=True)
    y = cen * lax.rsqrt(var + EPS) * cg_ref[...] + cb_ref[...]
    d_ref[...] = (y * _sigmoid(y)).astype(d_ref.dtype)

    ext_p[:HALO, :] = ext_p[tm:, :]
    ext_u[:HALO, :] = ext_u[tm:, :]


def _l1_mixer(x2, g, w, pw, ps, dw, db, cg, cb, batch, seq, tm):
    n = x2.shape[0]
    nt = seq // tm
    row = lambda b, t: (b * nt + t, 0)
    return pl.pallas_call(
        _l1_mixer_kernel,
        grid=(batch, nt),
        in_specs=[
            pl.BlockSpec((tm, D_MODEL), row),
            _const_spec((1, D_MODEL)),
            _const_spec((D_MODEL, ODD_IN_WIDTH)),
            _const_spec((len(POOL_WINDOWS), POOL_GROUP_DIM, POOL_GROUP_DIM)),
            _const_spec((1, POOL_WIDTH)),
            _const_spec((CONV_KERNEL, SUBLANES, CONV_WIDTH)),
            _const_spec((1, CONV_WIDTH)),
            _const_spec((1, CONV_WIDTH)),
            _const_spec((1, CONV_WIDTH)),
        ],
        out_specs=[pl.BlockSpec((tm, POOL_WIDTH), row), pl.BlockSpec((tm, CONV_WIDTH), row)],
        out_shape=[jax.ShapeDtypeStruct((n, POOL_WIDTH), BF16),
                   jax.ShapeDtypeStruct((n, CONV_WIDTH), BF16)],
        scratch_shapes=[pltpu.VMEM((HALO + tm, POOL_WIDTH), F32),
                        pltpu.VMEM((HALO + tm, CONV_WIDTH), F32),
                        pltpu.VMEM((SUBLANES - 1, HALO + tm, CONV_WIDTH), F32),
                        pltpu.VMEM((tm, CONV_WIDTH), F32)],
        compiler_params=_params("arbitrary", "arbitrary"),
        name="l1_mixer",
    )(x2, g, w, pw, ps, dw, db, cg, cb)


def _rope_tables(seq):
    half = HEAD_DIM // 2
    inv = jnp.power(ROPE_THETA, -jnp.arange(0, HEAD_DIM, 2, dtype=F32) / HEAD_DIM)
    ang = jnp.arange(seq, dtype=jnp.int32).astype(F32)[:, None] * inv[None, :]
    cos = jnp.cos(ang)
    sin = jnp.sin(ang)
    reps = LANES // HEAD_DIM
    return (jnp.tile(jnp.concatenate([cos, cos], axis=1), (1, reps)),
            jnp.tile(jnp.concatenate([-sin, sin], axis=1), (1, reps)))


def _tile(seq, want):
    return want if seq % want == 0 else seq


def kernel(x, mix_norm_0, w_in_0, lambda_q1_0, lambda_k1_0, lambda_q2_0, lambda_k2_0, subln_0, w_out_0, ffn_norm_0, w_gate_0, w_up_0, w_down_0, mix_norm_1, w_in_1, pool_w_1, pool_scale_1, dw_w_1, dw_b_1, conv_norm_g_1, conv_norm_b_1, w_out_1, ffn_norm_1, w_gate_1, w_up_1, w_down_1, final_norm):
    batch, seq, d = x.shape
    assert d == D_MODEL and seq % LANES == 0
    x2 = x.reshape(batch * seq, d)
    row = lambda v: v.reshape(1, -1).astype(F32)
    tm = _tile(seq, 512)
    tm_ffn = _tile(seq, 1024)

    cos_t, sin_t = _rope_tables(seq)
    proj, svt, dvt = _in_proj0(x2, row(mix_norm_0), w_in_0.astype(BF16), cos_t, sin_t,
                               batch, seq, tm)
    a_out = _sb_attention(proj, svt, batch, seq, _tile(seq, 256))
    lambda_init = 0.8 - 0.6 * math.exp(-0.3 * 0)
    b_out = _diff_attention(proj, dvt, row(lambda_q1_0), row(lambda_k1_0), row(lambda_q2_0),
                            row(lambda_k2_0), row(subln_0), batch, seq, _tile(seq, 512),
                            lambda_init)
    x2 = _out_ffn(x2, a_out, b_out, w_out_0.astype(BF16), row(ffn_norm_0),
                  w_gate_0.astype(BF16), w_up_0.astype(BF16), w_down_0.astype(BF16),
                  row(final_norm), tm_ffn, final_norm=False)

    c_out, d_out = _l1_mixer(x2, row(mix_norm_1), w_in_1.astype(BF16), pool_w_1.astype(BF16),
                             row(pool_scale_1),
                             jnp.broadcast_to(dw_w_1.astype(F32)[:, None, :],
                                              (CONV_KERNEL, SUBLANES, CONV_WIDTH)),
                             row(dw_b_1),
                             row(conv_norm_g_1), row(conv_norm_b_1), batch, seq, tm)
    out = _out_ffn(x2, c_out, d_out, w_out_1.astype(BF16), row(ffn_norm_1),
                   w_gate_1.astype(BF16), w_up_1.astype(BF16), w_down_1.astype(BF16),
                   row(final_norm), tm_ffn, final_norm=True)
    return out.reshape(batch, seq, d)
```
